```python
import math
import jax, jax.numpy as jnp
from jax import lax
import numpy as np

D_MODEL = 4096
BATCH = 4
SEQ = 2048
DEPTH = 4
DEC_BATCH = 128
DEC_SEQ = 4
PAST_LEN = 8192
PAGE_SIZE = 128

HEAD_DIM = 128
HQ_A = 16
HKV_A = 4
A_PATTERNS = ((128, 1), (512, 4), (2048, 16))
WIN_A_MAX = 2048
HQ_B = 16
HKV_B = 2
WIN_B = 128
H_C = 32
Q_LORA = 1024
KV_LORA = 512
QK_NOPE = 128
QK_ROPE = 64
V_DIM = 128
MLA_SCALE = (QK_NOPE + QK_ROPE) ** -0.5
D_FF = 4 * D_MODEL
ROPE_THETA = 10000.0
EPS = 1e-6
BLOCK = 128
N_EVEN = (DEPTH + 1) // 2
N_ODD = DEPTH // 2
IN_E = (HQ_A + 2 * HKV_A + HQ_B + 2 * HKV_B) * HEAD_DIM
OUT_E = (HQ_A + HQ_B) * HEAD_DIM

kernel_name = 'hybrid_dilated_swa_mla_decoder_step'


def rms_norm(x, g):
    xf = x.astype(jnp.float32)
    y = xf * lax.rsqrt(jnp.mean(xf * xf, axis=-1, keepdims=True) + EPS)
    return (y * g.astype(jnp.float32)).astype(x.dtype)


def rope_tables(pos, dim):
    inv = ROPE_THETA ** (-jnp.arange(0, dim, 2, dtype=jnp.float32) / dim)
    ang = pos.astype(jnp.float32)[:, None] * inv[None, :]
    return jnp.cos(ang), jnp.sin(ang)


def apply_rope(x, cos, sin):
    half = x.shape[-1] // 2
    xf = x.astype(jnp.float32)
    x1, x2 = xf[..., :half], xf[..., half:]
    c, s = cos[:, None, :], sin[:, None, :]
    return jnp.concatenate([x1 * c - x2 * s, x2 * c + x1 * s], axis=-1).astype(x.dtype)


def sq_relu_mlp(h, w_up, w_down):
    return jnp.square(jax.nn.relu(h @ w_up)) @ w_down


def banded_attention(q, k, v, window, sink=None):
    n, l, hq, dh = q.shape
    hkv = k.shape[2]
    g = hq // hkv
    nb = -(-l // BLOCK)
    lp = nb * BLOCK
    qb = jnp.pad(q, ((0, 0), (0, lp - l), (0, 0), (0, 0))).reshape(n, nb, BLOCK, hkv, g, dh)

    def band(t):
        tp = jnp.pad(t, ((0, 0), (BLOCK, lp - l), (0, 0), (0, 0)))
        prev = tp[:, :lp].reshape(n, nb, BLOCK, hkv, dh)
        cur = tp[:, BLOCK:].reshape(n, nb, BLOCK, hkv, dh)
        return jnp.concatenate([prev, cur], axis=2)

    kb, vb = band(k), band(v)
    s = jnp.einsum('nbqhgd,nbkhd->nbhgqk', qb, kb, preferred_element_type=jnp.float32) * (dh ** -0.5)
    qi = jnp.arange(BLOCK)[:, None]
    kj = jnp.arange(2 * BLOCK)[None, :]
    kpos = (jnp.arange(nb) * BLOCK - BLOCK)[:, None, None] + kj[None]
    dist = qi + BLOCK - kj
    mask = (dist >= 0) & (dist <= window) & (kpos >= 0)
    s = jnp.where(mask[None, :, None, None], s, -jnp.inf)
    m = jnp.max(s, axis=-1, keepdims=True)
    if sink is not None:
        sk = sink.astype(jnp.float32).reshape(1, 1, hkv, g, 1, 1)
        m = jnp.maximum(m, sk)
    p = jnp.exp(s - m)
    den = jnp.sum(p, axis=-1, keepdims=True)
    if sink is not None:
        den = den + jnp.exp(sk - m)
    o = jnp.einsum('nbhgqk,nbkhd->nbqhgd', (p / den).astype(v.dtype), vb, preferred_element_type=jnp.float32)
    lse = (m + jnp.log(den))[..., 0]
    o = o.reshape(n, lp, hq, dh)[:, :l]
    lse = lse.transpose(0, 1, 4, 2, 3).reshape(n, lp, hq)[:, :l]
    return o, lse


def dilated_attention_prompt(q, k, v):
    b, s, hq, dh = q.shape
    outs, lses = [], []
    for (w, d) in A_PATTERNS:
        ls = s // d

        def split(t):
            return t.reshape(b, ls, d, t.shape[2], dh).transpose(0, 2, 1, 3, 4).reshape(b * d, ls, t.shape[2], dh)

        o, lse = banded_attention(split(q), split(k), split(v), w // d)
        outs.append(o.reshape(b, d, ls, hq, dh).transpose(0, 2, 1, 3, 4).reshape(b, s, hq, dh))
        lses.append(lse.reshape(b, d, ls, hq).transpose(0, 2, 1, 3).reshape(b, s, hq))
    wts = jax.nn.softmax(jnp.stack(lses, 0), axis=0)
    return jnp.einsum('pbsh,pbshd->bshd', wts, jnp.stack(outs, 0)).astype(q.dtype)


def dilated_attention_sample(q, k_new, v_new, k_buf, v_buf):
    db, t, hq, dh = q.shape
    hkv = k_new.shape[2]
    g = hq // hkv
    lb = k_buf.shape[1]
    k_all = jnp.concatenate([k_buf, k_new], axis=1)
    v_all = jnp.concatenate([v_buf, v_new], axis=1)
    qg = q.reshape(db, t, hkv, g, dh)
    outs, lses = [], []
    for (w, d) in A_PATTERNS:
        steps = w // d
        idx = lb + jnp.arange(t)[:, None] - d * jnp.arange(steps + 1)[None, :]
        valid = idx >= 0
        idx = jnp.maximum(idx, 0)
        kg = jnp.take(k_all, idx, axis=1)
        vg = jnp.take(v_all, idx, axis=1)
        s = jnp.einsum('bthgd,btjhd->bthgj', qg, kg, preferred_element_type=jnp.float32) * (dh ** -0.5)
        s = jnp.where(valid[None, :, None, None, :], s, -jnp.inf)
        lse = jax.nn.logsumexp(s, axis=-1)
        p = jnp.exp(s - lse[..., None])
        o = jnp.einsum('bthgj,btjhd->bthgd', p.astype(v_all.dtype), vg, preferred_element_type=jnp.float32)
        outs.append(o.reshape(db, t, hq, dh))
        lses.append(lse.reshape(db, t, hq))
    wts = jax.nn.softmax(jnp.stack(lses, 0), axis=0)
    return jnp.einsum('pbsh,pbshd->bshd', wts, jnp.stack(outs, 0)).astype(q.dtype)


def swa_sink_sample(q, k_new, v_new, k_buf, v_buf, sink):
    db, t, hq, dh = q.shape
    hkv = k_new.shape[2]
    g = hq // hkv
    lb = k_buf.shape[1]
    k_all = jnp.concatenate([k_buf, k_new], axis=1)
    v_all = jnp.concatenate([v_buf, v_new], axis=1)
    dist = (lb + jnp.arange(t))[:, None] - jnp.arange(lb + t)[None, :]
    mask = (dist >= 0) & (dist <= WIN_B)
    s = jnp.einsum('bthgd,bkhd->bhgtk', q.reshape(db, t, hkv, g, dh), k_all,
                   preferred_element_type=jnp.float32) * (dh ** -0.5)
    s = jnp.where(mask, s, -jnp.inf)
    sk = sink.astype(jnp.float32).reshape(1, hkv, g, 1, 1)
    m = jnp.maximum(jnp.max(s, axis=-1, keepdims=True), sk)
    p = jnp.exp(s - m)
    den = jnp.sum(p, axis=-1, keepdims=True) + jnp.exp(sk - m)
    o = jnp.einsum('bhgtk,bkhd->bthgd', (p / den).astype(v_all.dtype), v_all, preferred_element_type=jnp.float32)
    return o.reshape(db, t, hq, dh).astype(q.dtype)


def project_even(h, w_in, cos, sin):
    b, l, _ = h.shape
    proj = h @ w_in
    sizes = (HQ_A, HKV_A, HKV_A, HQ_B, HKV_B, HKV_B)
    offs = [int(o) for o in np.cumsum([sz * HEAD_DIM for sz in sizes])[:-1]]
    qa, ka, va, qb, kb, vb = [t.reshape(b, l, -1, HEAD_DIM) for t in jnp.split(proj, offs, axis=-1)]
    return (apply_rope(qa, cos, sin), apply_rope(ka, cos, sin), va,
            apply_rope(qb, cos, sin), apply_rope(kb, cos, sin), vb)


def mla_queries(h, w_dq, q_norm, w_uq, cos, sin):
    b, l, _ = h.shape
    cq = rms_norm(h @ w_dq, q_norm)
    q = (cq @ w_uq).reshape(b, l, H_C, QK_NOPE + QK_ROPE)
    return q[..., :QK_NOPE], apply_rope(q[..., QK_NOPE:], cos, sin)


def mla_latent(h, w_dkv, kv_norm, cos, sin):
    ckv_pe = h @ w_dkv
    ckv = rms_norm(ckv_pe[..., :KV_LORA], kv_norm)
    kpe = apply_rope(ckv_pe[..., None, KV_LORA:], cos, sin)[..., 0, :]
    return ckv, kpe


def mla_prompt(q_nope, q_pe, ckv, kpe, w_ukv):
    b, s = ckv.shape[:2]
    kv = (ckv @ w_ukv).reshape(b, s, H_C, QK_NOPE + V_DIM)
    k_nope, v = kv[..., :QK_NOPE], kv[..., QK_NOPE:]
    nb = s // BLOCK
    kpos = jnp.arange(s)

    def to_blocks(t):
        return t.reshape(b, nb, BLOCK, *t.shape[2:]).swapaxes(0, 1)

    def one_block(args):
        qn, qp, i = args
        sc = (jnp.einsum('bqhd,bkhd->bhqk', qn, k_nope, preferred_element_type=jnp.float32)
              + jnp.einsum('bqhd,bkd->bhqk', qp, kpe, preferred_element_type=jnp.float32)) * MLA_SCALE
        qpos = i * BLOCK + jnp.arange(BLOCK)
        sc = jnp.where((kpos[None, :] <= qpos[:, None])[None, None], sc, -jnp.inf)
        p = jax.nn.softmax(sc, axis=-1).astype(v.dtype)
        return jnp.einsum('bhqk,bkhd->bqhd', p, v)

    o = lax.map(one_block, (to_blocks(q_nope), to_blocks(q_pe), jnp.arange(nb)))
    return o.swapaxes(0, 1).reshape(b, s, H_C * V_DIM)


def mla_sample(q_nope, q_pe, ckv_new, kpe_new, pool_ckv, pool_kpe, li, page_table, w_ukv):
    db, t = q_nope.shape[:2]
    w = w_ukv.reshape(KV_LORA, H_C, QK_NOPE + V_DIM)
    w_uk, w_uv = w[..., :QK_NOPE], w[..., QK_NOPE:]
    q_lat = jnp.einsum('bthd,chd->bthc', q_nope, w_uk)

    def one_seq(args):
        ql, qp, cn, kn, pages = args
        ckv_all = jnp.concatenate([pool_ckv[li, pages].reshape(-1, KV_LORA), cn], axis=0)
        kpe_all = jnp.concatenate([pool_kpe[li, pages].reshape(-1, QK_ROPE), kn], axis=0)
        p_len = ckv_all.shape[0] - t
        sc = (jnp.einsum('thc,kc->htk', ql, ckv_all, preferred_element_type=jnp.float32)
              + jnp.einsum('thd,kd->htk', qp, kpe_all, preferred_element_type=jnp.float32)) * MLA_SCALE
        mask = jnp.arange(p_len + t)[None, :] <= p_len + jnp.arange(t)[:, None]
        sc = jnp.where(mask[None], sc, -jnp.inf)
        p = jax.nn.softmax(sc, axis=-1).astype(ckv_all.dtype)
        return jnp.einsum('htk,kc->thc', p, ckv_all)

    o_lat = lax.map(one_seq, (q_lat, q_pe, ckv_new, kpe_new, page_table))
    return jnp.einsum('bthc,chd->bthd', o_lat, w_uv).reshape(db, t, H_C * V_DIM)


def setup_inputs(seed: int = 0) -> dict:
    key = jax.random.key(seed)
    ks = jax.random.split(key, 32)
    f32 = jnp.float32
    n_pages = PAST_LEN // PAGE_SIZE
    n_pool = (DEC_BATCH * n_pages * 5) // 4
    lbuf_a = min(WIN_A_MAX, PAST_LEN)
    lbuf_b = min(WIN_B, PAST_LEN)

    def dense(k, shape):
        return jax.random.normal(k, shape, f32) * shape[-2] ** -0.5

    def gain(k, shape):
        return 1.0 + 0.05 * jax.random.normal(k, shape, f32)

    perm = jax.random.permutation(ks[8], n_pool)
    page_table = perm[:DEC_BATCH * n_pages].reshape(DEC_BATCH, n_pages).astype(jnp.int32)
    return {
        'x_prompt': jax.random.normal(ks[0], (BATCH, SEQ, D_MODEL), f32),
        'x_sample': jax.random.normal(ks[1], (DEC_BATCH, DEC_SEQ, D_MODEL), f32),
        'cache_a_k': jax.random.normal(ks[2], (N_EVEN, DEC_BATCH, lbuf_a, HKV_A, HEAD_DIM), f32),
        'cache_a_v': jax.random.normal(ks[3], (N_EVEN, DEC_BATCH, lbuf_a, HKV_A, HEAD_DIM), f32),
        'cache_b_k': jax.random.normal(ks[4], (N_EVEN, DEC_BATCH, lbuf_b, HKV_B, HEAD_DIM), f32),
        'cache_b_v': jax.random.normal(ks[5], (N_EVEN, DEC_BATCH, lbuf_b, HKV_B, HEAD_DIM), f32),
        'cache_mla_ckv': jax.random.normal(ks[6], (N_ODD, n_pool, PAGE_SIZE, KV_LORA), f32),
        'cache_mla_kpe': jax.random.normal(ks[7], (N_ODD, n_pool, PAGE_SIZE, QK_ROPE), f32),
        'page_table': page_table,
        'norm_mix_pre': gain(ks[9], (DEPTH, D_MODEL)),
        'norm_mix_post': gain(ks[10], (DEPTH, D_MODEL)),
        'norm_ffn_pre': gain(ks[11], (DEPTH, D_MODEL)),
        'norm_ffn_post': gain(ks[12], (DEPTH, D_MODEL)),
        'w_in_e': dense(ks[13], (N_EVEN, D_MODEL, IN_E)),
        'sink_b': 0.5 * jax.random.normal(ks[14], (N_EVEN, HQ_B), f32),
        'w_out_e': dense(ks[15], (N_EVEN, OUT_E, D_MODEL)),
        'w_dq': dense(ks[16], (N_ODD, D_MODEL, Q_LORA)),
        'q_norm': gain(ks[17], (N_ODD, Q_LORA)),
        'w_uq': dense(ks[18], (N_ODD, Q_LORA, H_C * (QK_NOPE + QK_ROPE))),
        'w_dkv': dense(ks[19], (N_ODD, D_MODEL, KV_LORA + QK_ROPE)),
        'kv_norm': gain(ks[20], (N_ODD, KV_LORA)),
        'w_ukv': dense(ks[21], (N_ODD, KV_LORA, H_C * (QK_NOPE + V_DIM))),
        'w_o_mla': dense(ks[22], (N_ODD, H_C * V_DIM, D_MODEL)),
        'w_up': dense(ks[23], (DEPTH, D_MODEL, D_FF)),
        'w_down': dense(ks[24], (DEPTH, D_FF, D_MODEL)),
    }


def reference(x_prompt, x_sample, cache_a_k, cache_a_v, cache_b_k, cache_b_v, cache_mla_ckv, cache_mla_kpe,
              page_table, norm_mix_pre, norm_mix_post, norm_ffn_pre, norm_ffn_post, w_in_e, sink_b, w_out_e,
              w_dq, q_norm, w_uq, w_dkv, kv_norm, w_ukv, w_o_mla, w_up, w_down):
    pos_p = jnp.arange(SEQ)
    pos_s = PAST_LEN + jnp.arange(DEC_SEQ)
    cos_p, sin_p = rope_tables(pos_p, HEAD_DIM)
    cos_s, sin_s = rope_tables(pos_s, HEAD_DIM)
    cos_pm, sin_pm = rope_tables(pos_p, QK_ROPE)
    cos_sm, sin_sm = rope_tables(pos_s, QK_ROPE)
    keep_a = min(WIN_A_MAX, SEQ)
    keep_b = min(WIN_B, SEQ)
    hp, hs = x_prompt, x_sample
    p_ak, p_av, p_bk, p_bv, p_ckv, p_kpe = [], [], [], [], [], []
    s_ak, s_av, s_bk, s_bv, s_ckv, s_kpe = [], [], [], [], [], []
    for layer in range(DEPTH):
        li = layer // 2
        ap = rms_norm(hp, norm_mix_pre[layer])
        as_ = rms_norm(hs, norm_mix_pre[layer])
        if layer % 2 == 0:
            qa, ka, va, qb, kb, vb = project_even(ap, w_in_e[li], cos_p, sin_p)
            oa = dilated_attention_prompt(qa, ka, va)
            ob = banded_attention(qb, kb, vb, WIN_B, sink_b[li])[0].astype(qb.dtype)
            mix_p = jnp.concatenate([oa, ob], axis=2).reshape(BATCH, SEQ, OUT_E) @ w_out_e[li]
            p_ak.append(ka[:, SEQ - keep_a:]); p_av.append(va[:, SEQ - keep_a:])
            p_bk.append(kb[:, SEQ - keep_b:]); p_bv.append(vb[:, SEQ - keep_b:])
            qa, ka, va, qb, kb, vb = project_even(as_, w_in_e[li], cos_s, sin_s)
            oa = dilated_attention_sample(qa, ka, va, cache_a_k[li], cache_a_v[li])
            ob = swa_sink_sample(qb, kb, vb, cache_b_k[li], cache_b_v[li], sink_b[li])
            mix_s = jnp.concatenate([oa, ob], axis=2).reshape(DEC_BATCH, DEC_SEQ, OUT_E) @ w_out_e[li]
            s_ak.append(ka); s_av.append(va); s_bk.append(kb); s_bv.append(vb)
        else:
            qn, qp = mla_queries(ap, w_dq[li], q_norm[li], w_uq[li], cos_pm, sin_pm)
            ckv, kpe = mla_latent(ap, w_dkv[li], kv_norm[li], cos_pm, sin_pm)
            mix_p = mla_prompt(qn, qp, ckv, kpe, w_ukv[li]) @ w_o_mla[li]
            p_ckv.append(ckv); p_kpe.append(kpe)
            qn, qp = mla_queries(as_, w_dq[li], q_norm[li], w_uq[li], cos_sm, sin_sm)
            ckv, kpe = mla_latent(as_, w_dkv[li], kv_norm[li], cos_sm, sin_sm)
            mix_s = mla_sample(qn, qp, ckv, kpe, cache_mla_ckv, cache_mla_kpe, li, page_table, w_ukv[li]) @ w_o_mla[li]
            s_ckv.append(ckv); s_kpe.append(kpe)
        hp = hp + rms_norm(mix_p, norm_mix_post[layer])
        hs = hs + rms_norm(mix_s, norm_mix_post[layer])
        hp = hp + rms_norm(sq_relu_mlp(rms_norm(hp, norm_ffn_pre[layer]), w_up[layer], w_down[layer]), norm_ffn_post[layer])
        hs = hs + rms_norm(sq_relu_mlp(rms_norm(hs, norm_ffn_pre[layer]), w_up[layer], w_down[layer]), norm_ffn_post[layer])
    return (hp, hs,
            jnp.stack(p_ak), jnp.stack(p_av), jnp.stack(p_bk), jnp.stack(p_bv), jnp.stack(p_ckv), jnp.stack(p_kpe),
            jnp.stack(s_ak), jnp.stack(s_av), jnp.stack(s_bk), jnp.stack(s_bv), jnp.stack(s_ckv), jnp.stack(s_kpe))
```

```python
import functools

import numpy as np
import jax
import jax.numpy as jnp
from jax import lax
from jax.experimental import pallas as pl
from jax.experimental.pallas import tpu as pltpu

F32 = jnp.float32
BF16 = jnp.bfloat16

HEAD_DIM = 128
LANES = 128
HQ_A, HKV_A = 16, 4
HQ_B, HKV_B = 16, 2
A_PATTERNS = ((128, 1), (512, 4), (2048, 16))
WIN_B = 128
H_C = 32
Q_LORA, KV_LORA = 1024, 512
QK_NOPE, QK_ROPE, V_DIM = 128, 64, 128
MLA_SCALE = (QK_NOPE + QK_ROPE) ** -0.5
ATTN_SCALE = HEAD_DIM ** -0.5
ROPE_THETA = 10000.0
EPS = 1e-6
NEG_BIG = -1e30
VMEM_LIMIT = 56 * 1024 * 1024


def _params(n_axes):
    return pltpu.CompilerParams(dimension_semantics=("arbitrary",) * n_axes, vmem_limit_bytes=VMEM_LIMIT)


def _tile(n, t):
    t = min(n, t)
    assert n % t == 0, (n, t)
    return t


def _rms(x, g):
    return x * lax.rsqrt(jnp.mean(x * x, axis=-1, keepdims=True) + EPS) * g


def _lane_tile(x, width):
    reps = width // LANES
    return x if reps == 1 else jnp.concatenate([x] * reps, axis=1)


def _norm_cast_kernel(x_ref, g_ref, o_ref):
    o_ref[...] = _rms(x_ref[...], g_ref[...]).astype(o_ref.dtype)


def norm_cast(x, g):
    r, d = x.shape
    tr = _tile(r, 256)
    return pl.pallas_call(
        _norm_cast_kernel,
        grid=(r // tr,),
        in_specs=[pl.BlockSpec((tr, d), lambda i: (i, 0)), pl.BlockSpec((1, d), lambda i: (0, 0))],
        out_specs=pl.BlockSpec((tr, d), lambda i: (i, 0)),
        out_shape=jax.ShapeDtypeStruct((r, d), BF16),
        compiler_params=_params(1),
        name="norm_cast",
    )(x, g.reshape(1, d))


def _resid_norm_kernel(h_ref, m_ref, gp_ref, gn_ref, ho_ref, xn_ref):
    h = h_ref[...] + _rms(m_ref[...], gp_ref[...])
    ho_ref[...] = h
    xn_ref[...] = _rms(h, gn_ref[...]).astype(xn_ref.dtype)


def _resid_kernel(h_ref, m_ref, gp_ref, ho_ref):
    ho_ref[...] = h_ref[...] + _rms(m_ref[...], gp_ref[...])


def resid_norm(h, m, g_post, g_next):
    r, d = h.shape
    tr = _tile(r, 256)
    row = pl.BlockSpec((tr, d), lambda i: (i, 0))
    vec = pl.BlockSpec((1, d), lambda i: (0, 0))
    if g_next is None:
        return pl.pallas_call(
            _resid_kernel, grid=(r // tr,), in_specs=[row, row, vec], out_specs=row,
            out_shape=jax.ShapeDtypeStruct((r, d), F32), compiler_params=_params(1), name="resid",
        )(h, m, g_post.reshape(1, d)), None
    return pl.pallas_call(
        _resid_norm_kernel, grid=(r // tr,), in_specs=[row, row, vec, vec], out_specs=[row, row],
        out_shape=[jax.ShapeDtypeStruct((r, d), F32), jax.ShapeDtypeStruct((r, d), BF16)],
        compiler_params=_params(1), name="resid_norm",
    )(h, m, g_post.reshape(1, d), g_next.reshape(1, d))


def _matmul_kernel(x_ref, w_ref, o_ref):
    o_ref[...] = jnp.dot(x_ref[...].astype(BF16), w_ref[...], preferred_element_type=F32).astype(o_ref.dtype)


def matmul(x, w, out_dtype, tm=1024, tn=512):
    m, k = x.shape
    n = w.shape[1]
    tm, tn = _tile(m, tm), _tile(n, tn)
    return pl.pallas_call(
        _matmul_kernel,
        grid=(m // tm, n // tn),
        in_specs=[pl.BlockSpec((tm, k), lambda i, j: (i, 0)), pl.BlockSpec((k, tn), lambda i, j: (0, j))],
        out_specs=pl.BlockSpec((tm, tn), lambda i, j: (i, j)),
        out_shape=jax.ShapeDtypeStruct((m, n), out_dtype),
        compiler_params=_params(2),
        name="matmul",
    )(x, w)


def _matmul_rope_kernel(types_ref, x_ref, w_ref, cos_ref, sin_ref, o_ref, *, heads_per_block):
    j = pl.program_id(1)
    acc = jnp.dot(x_ref[...].astype(BF16), w_ref[...], preferred_element_type=F32)
    for hh in range(heads_per_block):
        t = types_ref[j * heads_per_block + hh]
        a = acc[:, hh * LANES:(hh + 1) * LANES]
        roped = a * cos_ref[t] + pltpu.roll(a, LANES // 2, 1) * sin_ref[t]
        o_ref[:, hh * LANES:(hh + 1) * LANES] = roped.astype(o_ref.dtype)


def matmul_rope(x, w, head_types, cos_tab, sin_tab, out_dtype, tm=1024, tn=512):
    m, k = x.shape
    n = w.shape[1]
    n_types, period, _ = cos_tab.shape
    tm, tn = _tile(m, tm), _tile(n, tn)
    tm = _tile(period, tm)
    assert m % tm == 0
    pblocks = period // tm
    tab_spec = pl.BlockSpec((n_types, tm, LANES), lambda i, j, t: (0, i % pblocks, 0))
    grid_spec = pltpu.PrefetchScalarGridSpec(
        num_scalar_prefetch=1,
        grid=(m // tm, n // tn),
        in_specs=[pl.BlockSpec((tm, k), lambda i, j, t: (i, 0)), pl.BlockSpec((k, tn), lambda i, j, t: (0, j)),
                  tab_spec, tab_spec],
        out_specs=pl.BlockSpec((tm, tn), lambda i, j, t: (i, j)),
    )
    return pl.pallas_call(
        functools.partial(_matmul_rope_kernel, heads_per_block=tn // LANES),
        grid_spec=grid_spec,
        out_shape=jax.ShapeDtypeStruct((m, n), out_dtype),
        compiler_params=_params(2),
        name="matmul_rope",
    )(head_types, x, w, cos_tab, sin_tab)


def _mlp_kernel(x_ref, wu_ref, wd_ref, o_ref):
    f = pl.program_id(1)
    h = jnp.dot(x_ref[...], wu_ref[...], preferred_element_type=F32)
    h = jnp.square(jnp.maximum(h, 0.0)).astype(BF16)
    part = jnp.dot(h, wd_ref[...], preferred_element_type=F32)

    @pl.when(f == 0)
    def _():
        o_ref[...] = part

    @pl.when(f > 0)
    def _():
        o_ref[...] += part


def mlp(xn, w_up, w_down, tm=512, tf=512):
    m, d = xn.shape
    ff = w_up.shape[1]
    tm, tf = _tile(m, tm), _tile(ff, tf)
    return pl.pallas_call(
        _mlp_kernel,
        grid=(m // tm, ff // tf),
        in_specs=[pl.BlockSpec((tm, d), lambda i, f: (i, 0)), pl.BlockSpec((d, tf), lambda i, f: (0, f)),
                  pl.BlockSpec((tf, d), lambda i, f: (f, 0))],
        out_specs=pl.BlockSpec((tm, d), lambda i, f: (i, 0)),
        out_shape=jax.ShapeDtypeStruct((m, d), F32),
        compiler_params=_params(2),
        name="mlp",
    )(xn, w_up, w_down)


def _mla_down_kernel(x_ref, w_ref, qg_ref, kvg_ref, cos_ref, sin_ref, cq_ref, ckv_ref, kpe_ref):
    acc = jnp.dot(x_ref[...], w_ref[...], preferred_element_type=F32)
    cq_ref[...] = _rms(acc[:, :Q_LORA], qg_ref[...]).astype(cq_ref.dtype)
    ckv_ref[...] = _rms(acc[:, Q_LORA:Q_LORA + KV_LORA], kvg_ref[...])
    p = acc[:, Q_LORA + KV_LORA:]
    kpe_ref[...] = p * cos_ref[...] + pltpu.roll(p, LANES // 2, 1) * sin_ref[...]


def mla_down(xn, w_cat, q_gain, kv_gain, cos_tab, sin_tab, tm=512):
    m, d = xn.shape
    n = w_cat.shape[1]
    period = cos_tab.shape[0]
    tm = _tile(period, _tile(m, tm))
    pblocks = period // tm
    tab_spec = pl.BlockSpec((tm, LANES), lambda i: (i % pblocks, 0))
    return pl.pallas_call(
        _mla_down_kernel,
        grid=(m // tm,),
        in_specs=[pl.BlockSpec((tm, d), lambda i: (i, 0)), pl.BlockSpec((d, n), lambda i: (0, 0)),
                  pl.BlockSpec((1, Q_LORA), lambda i: (0, 0)), pl.BlockSpec((1, KV_LORA), lambda i: (0, 0)),
                  tab_spec, tab_spec],
        out_specs=[pl.BlockSpec((tm, Q_LORA), lambda i: (i, 0)), pl.BlockSpec((tm, KV_LORA), lambda i: (i, 0)),
                   pl.BlockSpec((tm, LANES), lambda i: (i, 0))],
        out_shape=[jax.ShapeDtypeStruct((m, Q_LORA), BF16), jax.ShapeDtypeStruct((m, KV_LORA), F32),
                   jax.ShapeDtypeStruct((m, LANES), F32)],
        compiler_params=_params(1),
        name="mla_down",
    )(xn, w_cat, q_gain.reshape(1, Q_LORA), kv_gain.reshape(1, KV_LORA), cos_tab, sin_tab)


def _online_softmax_step(s, v, m_sc, l_sc, acc_sc):
    tk = s.shape[1]
    m_prev = m_sc[...]
    m_new = jnp.maximum(m_prev, jnp.max(s, axis=-1, keepdims=True))
    alpha = jnp.exp(m_prev - m_new)
    p = jnp.exp(s - _lane_tile(m_new, tk))
    l_sc[...] = alpha * l_sc[...] + jnp.sum(p, axis=-1, keepdims=True)
    pv = jnp.dot(p.astype(BF16), v, preferred_element_type=F32)
    acc_sc[...] = acc_sc[...] * _lane_tile(alpha, pv.shape[1]) + pv
    m_sc[...] = m_new


def _band_attn_kernel(sink_ref, q_ref, k_ref, v_ref, bias_ref, o_ref, m_sc, l_sc, acc_sc, *,
                      group, tq, tk, n_delta, has_sink):
    hk = pl.program_id(1)
    qb = pl.program_id(2)
    q = jnp.concatenate([q_ref[:, g * HEAD_DIM:(g + 1) * HEAD_DIM] for g in range(group)], axis=0).astype(BF16)

    if has_sink:
        for g in range(group):
            m_sc[g * tq:(g + 1) * tq, :] = jnp.full((tq, LANES), sink_ref[hk * group + g], F32)
        l_sc[...] = jnp.ones_like(l_sc)
    else:
        m_sc[...] = jnp.full_like(m_sc, NEG_BIG)
        l_sc[...] = jnp.zeros_like(l_sc)
    acc_sc[...] = jnp.zeros_like(acc_sc)

    def body(d, carry):
        start = pl.multiple_of((qb - d) * tk, tk)
        k = k_ref[pl.ds(start, tk), :].astype(BF16)
        v = v_ref[pl.ds(start, tk), :].astype(BF16)
        s = lax.dot_general(q, k, (((1,), (1,)), ((), ())), preferred_element_type=F32) * ATTN_SCALE
        s = (s.reshape(group, tq, tk) + bias_ref[d][None]).reshape(group * tq, tk)
        _online_softmax_step(s, v, m_sc, l_sc, acc_sc)
        return carry

    lax.fori_loop(0, jnp.minimum(qb + 1, n_delta), body, 0)
    out = acc_sc[...] / l_sc[...]
    for g in range(group):
        o_ref[:, g * HEAD_DIM:(g + 1) * HEAD_DIM] = out[g * tq:(g + 1) * tq, :].astype(o_ref.dtype)


def _band_bias(tq, tk, n_delta, patterns):
    i = np.arange(tq)[None, :, None]
    j = np.arange(tk)[None, None, :]
    dist = np.arange(n_delta)[:, None, None] * tk + i - j
    count = np.zeros(dist.shape, np.float64)
    for (w, d) in patterns:
        count += (dist >= 0) & (dist <= w) & (dist % d == 0)
    with np.errstate(divide="ignore"):
        return np.log(count).astype(np.float32)


def band_attention(proj, seq, q_col, k_col, v_col, n_kv, group, patterns, sink):
    rows = proj.shape[0]
    batch = rows // seq
    tq = tk = _tile(seq, 256)
    nq = seq // tq
    reach = max(w for (w, _) in patterns)
    n_delta = min(nq, -(-reach // tk) + 1)
    bias = jnp.asarray(_band_bias(tq, tk, n_delta, patterns))
    has_sink = sink is not None
    if sink is None:
        sink = jnp.zeros((n_kv * group,), F32)
    gw = group * HEAD_DIM
    grid_spec = pltpu.PrefetchScalarGridSpec(
        num_scalar_prefetch=0,
        grid=(batch, n_kv, nq),
        in_specs=[
            pl.BlockSpec(memory_space=pltpu.SMEM),
            pl.BlockSpec((tq, gw), lambda b, h, i: (b * nq + i, q_col + h)),
            pl.BlockSpec((seq, HEAD_DIM), lambda b, h, i: (b, k_col + h)),
            pl.BlockSpec((seq, HEAD_DIM), lambda b, h, i: (b, v_col + h)),
            pl.BlockSpec((n_delta, tq, tk), lambda b, h, i: (0, 0, 0)),
        ],
        out_specs=pl.BlockSpec((tq, gw), lambda b, h, i: (b * nq + i, h)),
        scratch_shapes=[pltpu.VMEM((group * tq, LANES), F32), pltpu.VMEM((group * tq, LANES), F32),
                        pltpu.VMEM((group * tq, HEAD_DIM), F32)],
    )
    return pl.pallas_call(
        functools.partial(_band_attn_kernel, group=group, tq=tq, tk=tk, n_delta=n_delta, has_sink=has_sink),
        grid_spec=grid_spec,
        out_shape=jax.ShapeDtypeStruct((rows, n_kv * gw), BF16),
        compiler_params=_params(3),
        name="band_attention",
    )(sink.astype(F32), proj, proj, proj, bias)


def _mla_prompt_kernel(q_ref, kn_ref, v_ref, kpe_ref, o_ref, m_sc, l_sc, acc_sc, *, tq, tk):
    qb = pl.program_id(2)
    q = q_ref[...]
    m_sc[...] = jnp.full_like(m_sc, NEG_BIG)
    l_sc[...] = jnp.zeros_like(l_sc)
    acc_sc[...] = jnp.zeros_like(acc_sc)

    def scores(kb):
        start = pl.multiple_of(kb * tk, tk)
        k = jnp.concatenate([kn_ref[pl.ds(start, tk), :], kpe_ref[pl.ds(start, tk), :]], axis=1)
        s = lax.dot_general(q, k, (((1,), (1,)), ((), ())), preferred_element_type=F32) * MLA_SCALE
        return s, v_ref[pl.ds(start, tk), :]

    def body(kb, carry):
        s, v = scores(kb)
        _online_softmax_step(s, v, m_sc, l_sc, acc_sc)
        return carry

    lax.fori_loop(0, qb, body, 0)
    s, v = scores(qb)
    causal = lax.broadcasted_iota(jnp.int32, (tq, tk), 0) >= lax.broadcasted_iota(jnp.int32, (tq, tk), 1)
    _online_softmax_step(jnp.where(causal, s, -jnp.inf), v, m_sc, l_sc, acc_sc)
    o_ref[...] = (acc_sc[...] / l_sc[...]).astype(o_ref.dtype)


def mla_prompt_attention(q_full, kv, kpe, seq):
    rows = q_full.shape[0]
    batch = rows // seq
    tq = tk = _tile(seq, 512)
    nq = seq // tq
    return pl.pallas_call(
        functools.partial(_mla_prompt_kernel, tq=tq, tk=tk),
        grid=(batch, H_C, nq),
        in_specs=[
            pl.BlockSpec((tq, 2 * LANES), lambda b, h, i: (b * nq + i, h)),
            pl.BlockSpec((seq, LANES), lambda b, h, i: (b, 2 * h)),
            pl.BlockSpec((seq, LANES), lambda b, h, i: (b, 2 * h + 1)),
            pl.BlockSpec((seq, LANES), lambda b, h, i: (b, 0)),
        ],
        out_specs=pl.BlockSpec((tq, V_DIM), lambda b, h, i: (b * nq + i, h)),
        out_shape=jax.ShapeDtypeStruct((rows, H_C * V_DIM), BF16),
        scratch_shapes=[pltpu.VMEM((tq, LANES), F32), pltpu.VMEM((tq, LANES), F32), pltpu.VMEM((tq, V_DIM), F32)],
        compiler_params=_params(3),
        name="mla_prompt_attention",
    )(q_full, kv, kv, kpe)


def _sample_win_kernel(q_ref, kc_ref, vc_ref, kn_ref, vn_ref, bc_ref, bn_ref, m0_ref, l0_ref, o_ref, *,
                       n_kv, rows_per_kv):
    qblk = q_ref[0]
    s_c = jnp.dot(kc_ref[0].astype(BF16), qblk, preferred_element_type=F32) * ATTN_SCALE + bc_ref[...]
    s_n = jnp.dot(kn_ref[0].astype(BF16), qblk, preferred_element_type=F32) * ATTN_SCALE + bn_ref[...]
    m0 = m0_ref[...]
    m = jnp.maximum(jnp.maximum(jnp.max(s_c, axis=0, keepdims=True), jnp.max(s_n, axis=0, keepdims=True)), m0)
    p_c = jnp.exp(s_c - m)
    p_n = jnp.exp(s_n - m)
    den = (jnp.sum(p_c, axis=0, keepdims=True) + jnp.sum(p_n, axis=0, keepdims=True)
           + l0_ref[...] * jnp.exp(m0 - m))
    contract_keys = (((0,), (0,)), ((), ()))
    o = (lax.dot_general((p_c / den).astype(BF16), vc_ref[0].astype(BF16), contract_keys,
                         preferred_element_type=F32)
         + lax.dot_general((p_n / den).astype(BF16), vn_ref[0].astype(BF16), contract_keys,
                           preferred_element_type=F32))
    for h in range(n_kv):
        rows = slice(h * rows_per_kv, (h + 1) * rows_per_kv)
        o_ref[0, rows, :] = o[rows, h * HEAD_DIM:(h + 1) * HEAD_DIM]


def _sample_bias(n_cache, n_new_pad, steps, group, n_kv, patterns):
    cols = np.arange(LANES)
    t = (cols % (steps * group)) // group
    live = cols < n_kv * steps * group

    def table(pos):
        dist = (n_cache + t)[None, :] - pos[:, None]
        count = np.zeros(dist.shape, np.float64)
        for (w, d) in patterns:
            count += (dist >= 0) & (dist <= w) & (dist % d == 0)
        with np.errstate(divide="ignore"):
            b = np.log(count)
        return np.where(live[None, :], b, 0.0).astype(np.float32)

    new_pos = n_cache + np.arange(n_new_pad)
    bias_new = table(new_pos)
    bias_new[steps:, :] = -np.inf
    return table(np.arange(n_cache)), bias_new


def sample_window_attention(q, k_new, v_new, k_cache, v_cache, n_kv, group, patterns, sink):
    db, steps, hq, _ = q.shape
    n_cache = k_cache.shape[1]
    width = n_kv * HEAD_DIM
    rows_per_kv = steps * group
    n_cols = n_kv * rows_per_kv
    new_pad = 8
    q_t = q.reshape(db, steps, n_kv, group, HEAD_DIM).transpose(0, 2, 4, 1, 3).reshape(db, n_kv, HEAD_DIM, rows_per_kv)
    qblk = (q_t[:, :, :, None, :] * jnp.eye(n_kv, dtype=F32)[None, :, None, :, None]).reshape(db, width, n_cols)
    qblk = jnp.pad(qblk, ((0, 0), (0, 0), (0, LANES - n_cols))).astype(BF16)
    pad_new = lambda x: jnp.pad(x.reshape(db, steps, width), ((0, 0), (0, new_pad - steps), (0, 0)))
    bias_c, bias_n = _sample_bias(n_cache, new_pad, steps, group, n_kv, patterns)
    if sink is None:
        m0 = jnp.full((1, LANES), NEG_BIG, F32)
        l0 = jnp.zeros((1, LANES), F32)
    else:
        per_col = jnp.broadcast_to(sink.reshape(n_kv, 1, group), (n_kv, steps, group)).reshape(n_cols)
        m0 = jnp.pad(per_col.astype(F32), (0, LANES - n_cols)).reshape(1, LANES)
        l0 = jnp.ones((1, LANES), F32)
    seq_blk = lambda n: pl.BlockSpec((1, n, width), lambda b: (b, 0, 0))
    const = lambda shape: pl.BlockSpec(shape, lambda b: (0,) * len(shape))
    out = pl.pallas_call(
        functools.partial(_sample_win_kernel, n_kv=n_kv, rows_per_kv=rows_per_kv),
        grid=(db,),
        in_specs=[pl.BlockSpec((1, width, LANES), lambda b: (b, 0, 0)), seq_blk(n_cache), seq_blk(n_cache),
                  seq_blk(new_pad), seq_blk(new_pad), const((n_cache, LANES)), const((new_pad, LANES)),
                  const((1, LANES)), const((1, LANES))],
        out_specs=pl.BlockSpec((1, n_cols, HEAD_DIM), lambda b: (b, 0, 0)),
        out_shape=jax.ShapeDtypeStruct((db, n_cols, HEAD_DIM), F32),
        compiler_params=_params(1),
        name="sample_window_attention",
    )(qblk, k_cache.reshape(db, n_cache, width), v_cache.reshape(db, n_cache, width), pad_new(k_new), pad_new(v_new),
      jnp.asarray(bias_c), jnp.asarray(bias_n), m0, l0)
    out = out.reshape(db, n_kv, steps, group, HEAD_DIM).transpose(0, 2, 1, 3, 4)
    return out.reshape(db * steps, hq * HEAD_DIM)


def _head_matmul_kernel(x_ref, w_ref, o_ref, *, contract_w_cols):
    dims = (((1,), (1 if contract_w_cols else 0,)), ((), ()))
    o_ref[...] = lax.dot_general(x_ref[...], w_ref[...], dims, preferred_element_type=F32).astype(o_ref.dtype)


def head_matmul(x, w, x_width, x_stride, w_off, out_width, contract_w_cols):
    m = x.shape[0]
    return pl.pallas_call(
        functools.partial(_head_matmul_kernel, contract_w_cols=contract_w_cols),
        grid=(H_C,),
        in_specs=[pl.BlockSpec((m, x_width), lambda h: (0, h * x_stride)),
                  pl.BlockSpec((KV_LORA, LANES), lambda h: (0, 2 * h + w_off))],
        out_specs=pl.BlockSpec((m, out_width), lambda h: (0, h)),
        out_shape=jax.ShapeDtypeStruct((m, H_C * out_width), BF16),
        compiler_params=_params(1),
        name="head_matmul",
    )(x, w)


def _mla_sample_kernel(*refs, pages_per_step, page, steps):
    pt_ref, ql_ref, qp_ref, cn_ref, kn_ref = refs[:5]
    ckv_refs = refs[5:5 + pages_per_step]
    kpe_refs = refs[5 + pages_per_step:5 + 2 * pages_per_step]
    o_ref, m_sc, l_sc, acc_sc, ckv_sc, kpe_sc = refs[5 + 2 * pages_per_step:]
    del pt_ref
    c = pl.program_id(1)
    contract_last = (((1,), (1,)), ((), ()))

    @pl.when(c == 0)
    def _():
        m_sc[...] = jnp.full_like(m_sc, NEG_BIG)
        l_sc[...] = jnp.zeros_like(l_sc)
        acc_sc[...] = jnp.zeros_like(acc_sc)

    for k in range(pages_per_step):
        ckv_sc[k * page:(k + 1) * page, :] = ckv_refs[k][0, 0].astype(BF16)
        kpe_sc[k * page:(k + 1) * page, :] = kpe_refs[k][0, 0].astype(BF16)
    ql = ql_ref[0]
    qp = qp_ref[0]
    ckv = ckv_sc[...]
    s = (lax.dot_general(ql, ckv, contract_last, preferred_element_type=F32)
         + lax.dot_general(qp, kpe_sc[...], contract_last, preferred_element_type=F32)) * MLA_SCALE
    _online_softmax_step(s, ckv, m_sc, l_sc, acc_sc)

    @pl.when(c == pl.num_programs(1) - 1)
    def _():
        cn = cn_ref[0].astype(BF16)
        s_n = (lax.dot_general(ql, cn, contract_last, preferred_element_type=F32)
               + lax.dot_general(qp, kn_ref[0].astype(BF16), contract_last, preferred_element_type=F32)) * MLA_SCALE
        rows_per_step = ql.shape[0] // steps
        t_row = lax.broadcasted_iota(jnp.int32, s_n.shape, 0) // rows_per_step
        t_key = lax.broadcasted_iota(jnp.int32, s_n.shape, 1)
        s_n = jnp.where((t_key <= t_row) & (t_key < steps), s_n, -jnp.inf)
        _online_softmax_step(s_n, cn, m_sc, l_sc, acc_sc)
        o_ref[0] = (acc_sc[...] / _lane_tile(l_sc[...], KV_LORA)).astype(o_ref.dtype)


def mla_sample_attention(q_lat, q_pe, ckv_new, kpe_new, pool_ckv, pool_kpe, layer, page_table, steps):
    db, rows, _ = q_lat.shape
    n_pages = page_table.shape[1]
    page = pool_ckv.shape[2]
    pps = min(16, n_pages)
    assert n_pages % pps == 0
    new_pad = LANES
    pad_new = lambda x: jnp.pad(x, ((0, 0), (0, new_pad - steps), (0, 0)))

    def page_spec(width, slot):
        return pl.BlockSpec((1, 1, page, width), lambda b, c, pt: (layer, pt[b, c * pps + slot], 0, 0))

    per_seq = lambda n, w: pl.BlockSpec((1, n, w), lambda b, c, pt: (b, 0, 0))
    grid_spec = pltpu.PrefetchScalarGridSpec(
        num_scalar_prefetch=1,
        grid=(db, n_pages // pps),
        in_specs=([per_seq(rows, KV_LORA), per_seq(rows, QK_ROPE), per_seq(new_pad, KV_LORA), per_seq(new_pad, QK_ROPE)]
                  + [page_spec(KV_LORA, k) for k in range(pps)] + [page_spec(QK_ROPE, k) for k in range(pps)]),
        out_specs=per_seq(rows, KV_LORA),
        scratch_shapes=[pltpu.VMEM((rows, LANES), F32), pltpu.VMEM((rows, LANES), F32),
                        pltpu.VMEM((rows, KV_LORA), F32), pltpu.VMEM((pps * page, KV_LORA), BF16),
                        pltpu.VMEM((pps * page, QK_ROPE), BF16)],
    )
    return pl.pallas_call(
        functools.partial(_mla_sample_kernel, pages_per_step=pps, page=page, steps=steps),
        grid_spec=grid_spec,
        out_shape=jax.ShapeDtypeStruct((db, rows, KV_LORA), BF16),
        compiler_params=_params(2),
        name="mla_sample_attention",
    )(page_table, q_lat, q_pe, pad_new(ckv_new), pad_new(kpe_new), *([pool_ckv] * pps), *([pool_kpe] * pps))


def _rope_angles(pos, dim):
    inv = ROPE_THETA ** (-jnp.arange(0, dim, 2, dtype=F32) / dim)
    ang = pos.astype(F32)[:, None] * inv[None, :]
    return jnp.cos(ang), jnp.sin(ang)


def _head_rope_tables(pos):
    c, s = _rope_angles(pos, HEAD_DIM)
    cos = jnp.stack([jnp.concatenate([c, c], 1), jnp.ones((pos.shape[0], LANES), F32)])
    sin = jnp.stack([jnp.concatenate([-s, s], 1), jnp.zeros((pos.shape[0], LANES), F32)])
    return cos, sin


def _mla_rope_tables(pos):
    c, s = _rope_angles(pos, QK_ROPE)
    z = jnp.zeros_like(c)
    cos = jnp.stack([jnp.concatenate([c, z, c, z], 1), jnp.ones((pos.shape[0], LANES), F32)])
    sin = jnp.stack([jnp.concatenate([-s, z, s, z], 1), jnp.zeros((pos.shape[0], LANES), F32)])
    return cos, sin


def _spread_rope_cols(w):
    half = QK_ROPE // 2
    z = jnp.zeros(w.shape[:-1] + (half,), w.dtype)
    return jnp.concatenate([w[..., :half], z, w[..., half:], z], axis=-1)


def _gather_rope_cols(x):
    half = QK_ROPE // 2
    return jnp.concatenate([x[..., :half], x[..., 2 * half:3 * half]], axis=-1)


EVEN_HEAD_TYPES = np.array([0] * (HQ_A + HKV_A) + [1] * HKV_A + [0] * (HQ_B + HKV_B) + [1] * HKV_B, np.int32)
MLA_Q_HEAD_TYPES = np.array([1, 0] * H_C, np.int32)
COL_QA, COL_KA, COL_VA = 0, HQ_A, HQ_A + HKV_A
COL_QB, COL_KB, COL_VB = HQ_A + 2 * HKV_A, HQ_A + 2 * HKV_A + HQ_B, HQ_A + 2 * HKV_A + HQ_B + HKV_B


def _cols(x, first_head, n_heads):
    return x[:, first_head * HEAD_DIM:(first_head + n_heads) * HEAD_DIM]


def kernel(x_prompt, x_sample, cache_a_k, cache_a_v, cache_b_k, cache_b_v, cache_mla_ckv, cache_mla_kpe,
           page_table, norm_mix_pre, norm_mix_post, norm_ffn_pre, norm_ffn_post, w_in_e, sink_b, w_out_e,
           w_dq, q_norm, w_uq, w_dkv, kv_norm, w_ukv, w_o_mla, w_up, w_down):
    batch, seq, d_model = x_prompt.shape
    db, steps, _ = x_sample.shape
    depth = norm_mix_pre.shape[0]
    past_len = page_table.shape[1] * cache_mla_ckv.shape[2]
    keep_a = min(max(w for (w, _) in A_PATTERNS), seq)
    keep_b = min(WIN_B, seq)

    pos_p = jnp.arange(seq)
    pos_s = jnp.tile(past_len + jnp.arange(steps), db)
    rope_p, rope_s = _head_rope_tables(pos_p), _head_rope_tables(pos_s)
    mrope_p, mrope_s = _mla_rope_tables(pos_p), _mla_rope_tables(pos_s)
    even_types = jnp.asarray(EVEN_HEAD_TYPES)
    mla_q_types = jnp.asarray(MLA_Q_HEAD_TYPES)

    hp = x_prompt.reshape(batch * seq, d_model)
    hs = x_sample.reshape(db * steps, d_model)
    xp = norm_cast(hp, norm_mix_pre[0])
    xs = norm_cast(hs, norm_mix_pre[0])
    outs = {name: [] for name in ("p_ak", "p_av", "p_bk", "p_bv", "p_ckv", "p_kpe",
                                  "s_ak", "s_av", "s_bk", "s_bv", "s_ckv", "s_kpe")}

    for layer in range(depth):
        li = layer // 2
        if layer % 2 == 0:
            w_in = w_in_e[li].astype(BF16)
            w_out = w_out_e[li].astype(BF16)
            proj = matmul_rope(xp, w_in, even_types, *rope_p, F32)
            oa = band_attention(proj, seq, COL_QA // (HQ_A // HKV_A), COL_KA, COL_VA, HKV_A, HQ_A // HKV_A,
                                A_PATTERNS, None)
            ob = band_attention(proj, seq, COL_QB // (HQ_B // HKV_B), COL_KB, COL_VB, HKV_B, HQ_B // HKV_B,
                                ((WIN_B, 1),), sink_b[li])
            mix_p = matmul(jnp.concatenate([oa, ob], axis=1), w_out, F32)
            per_seq = lambda x, n: x.reshape(batch, seq, n, HEAD_DIM)
            outs["p_ak"].append(per_seq(_cols(proj, COL_KA, HKV_A), HKV_A)[:, seq - keep_a:])
            outs["p_av"].append(per_seq(_cols(proj, COL_VA, HKV_A), HKV_A)[:, seq - keep_a:])
            outs["p_bk"].append(per_seq(_cols(proj, COL_KB, HKV_B), HKV_B)[:, seq - keep_b:])
            outs["p_bv"].append(per_seq(_cols(proj, COL_VB, HKV_B), HKV_B)[:, seq - keep_b:])
            proj = matmul_rope(xs, w_in, even_types, *rope_s, F32)
            per_tok = lambda x, n: x.reshape(db, steps, n, HEAD_DIM)
            ka, va = per_tok(_cols(proj, COL_KA, HKV_A), HKV_A), per_tok(_cols(proj, COL_VA, HKV_A), HKV_A)
            kb, vb = per_tok(_cols(proj, COL_KB, HKV_B), HKV_B), per_tok(_cols(proj, COL_VB, HKV_B), HKV_B)
            oa = sample_window_attention(per_tok(_cols(proj, COL_QA, HQ_A), HQ_A), ka, va, cache_a_k[li],
                                         cache_a_v[li], HKV_A, HQ_A // HKV_A, A_PATTERNS, None)
            ob = sample_window_attention(per_tok(_cols(proj, COL_QB, HQ_B), HQ_B), kb, vb, cache_b_k[li],
                                         cache_b_v[li], HKV_B, HQ_B // HKV_B, ((WIN_B, 1),), sink_b[li])
            mix_s = matmul(jnp.concatenate([oa, ob], axis=1), w_out, F32)
            outs["s_ak"].append(ka); outs["s_av"].append(va); outs["s_bk"].append(kb); outs["s_bv"].append(vb)
        else:
            w_cat = jnp.concatenate([w_dq[li], w_dkv[li][:, :KV_LORA], _spread_rope_cols(w_dkv[li][:, KV_LORA:])],
                                    axis=1).astype(BF16)
            w_uq_l = w_uq[li].reshape(Q_LORA, H_C, QK_NOPE + QK_ROPE)
            w_q = jnp.concatenate([w_uq_l[..., :QK_NOPE], _spread_rope_cols(w_uq_l[..., QK_NOPE:])], axis=-1)
            w_q = w_q.reshape(Q_LORA, H_C * 2 * LANES).astype(BF16)
            w_kv = w_ukv[li].astype(BF16)
            w_o = w_o_mla[li].astype(BF16)
            cq, ckv, kpe = mla_down(xp, w_cat, q_norm[li], kv_norm[li], mrope_p[0][0], mrope_p[1][0])
            q_full = matmul_rope(cq, w_q, mla_q_types, *mrope_p, BF16, tn=1024)
            kv = matmul(ckv, w_kv, BF16, tn=1024)
            o = mla_prompt_attention(q_full, kv, kpe.astype(BF16), seq)
            mix_p = matmul(o, w_o, F32)
            outs["p_ckv"].append(ckv.reshape(batch, seq, KV_LORA))
            outs["p_kpe"].append(_gather_rope_cols(kpe).reshape(batch, seq, QK_ROPE))
            cq, ckv, kpe = mla_down(xs, w_cat, q_norm[li], kv_norm[li], mrope_s[0][0], mrope_s[1][0])
            q_full = matmul_rope(cq, w_q, mla_q_types, *mrope_s, BF16, tn=1024)
            q_lat = head_matmul(q_full, w_kv, LANES, 2, 0, KV_LORA, True)
            q_pe = _gather_rope_cols(q_full.reshape(db * steps, H_C, 2, LANES)[:, :, 1, :])
            ckv_s = ckv.reshape(db, steps, KV_LORA)
            kpe_s = _gather_rope_cols(kpe).reshape(db, steps, QK_ROPE)
            o_lat = mla_sample_attention(q_lat.reshape(db, steps * H_C, KV_LORA),
                                         q_pe.reshape(db, steps * H_C, QK_ROPE), ckv_s, kpe_s,
                                         cache_mla_ckv, cache_mla_kpe, li, page_table, steps)
            o = head_matmul(o_lat.reshape(db * steps, H_C * KV_LORA), w_kv, KV_LORA, 1, 1, V_DIM, False)
            mix_s = matmul(o, w_o, F32)
            outs["s_ckv"].append(ckv_s); outs["s_kpe"].append(kpe_s)

        hp, xp = resid_norm(hp, mix_p, norm_mix_post[layer], norm_ffn_pre[layer])
        hs, xs = resid_norm(hs, mix_s, norm_mix_post[layer], norm_ffn_pre[layer])
        w_u, w_d = w_up[layer].astype(BF16), w_down[layer].astype(BF16)
        g_next = norm_mix_pre[layer + 1] if layer + 1 < depth else None
        hp, xp = resid_norm(hp, mlp(xp, w_u, w_d), norm_ffn_post[layer], g_next)
        hs, xs = resid_norm(hs, mlp(xs, w_u, w_d), norm_ffn_post[layer], g_next)

    stack = lambda name: jnp.stack(outs[name])
    return (hp.reshape(batch, seq, d_model), hs.reshape(db, steps, d_model),
            stack("p_ak"), stack("p_av"), stack("p_bk"), stack("p_bv"), stack("p_ckv"), stack("p_kpe"),
            stack("s_ak"), stack("s_av"), stack("s_bk"), stack("s_bv"), stack("s_ckv"), stack("s_kpe"))
```

```python
import functools

import numpy as np
import jax
import jax.numpy as jnp
from jax import lax
from jax.experimental import pallas as pl
from jax.experimental.pallas import tpu as pltpu

F32 = jnp.float32
BF16 = jnp.bfloat16

HEAD_DIM = 128
LANES = 128
HQ_A, HKV_A = 16, 4
HQ_B, HKV_B = 16, 2
A_PATTERNS = ((128, 1), (512, 4), (2048, 16))
WIN_B = 128
H_C = 32
Q_LORA, KV_LORA = 1024, 512
QK_NOPE, QK_ROPE, V_DIM = 128, 64, 128
MLA_SCALE = (QK_NOPE + QK_ROPE) ** -0.5
ATTN_SCALE = HEAD_DIM ** -0.5
LOG2E = 1.4426950408889634
ROPE_THETA = 10000.0
EPS = 1e-6
NEG_BIG = -1e30
VMEM_LIMIT = 56 * 1024 * 1024


def _params(n_axes):
    return pltpu.CompilerParams(dimension_semantics=("arbitrary",) * n_axes, vmem_limit_bytes=VMEM_LIMIT)


def _tile(n, t):
    t = min(n, t)
    assert n % t == 0, (n, t)
    return t


def _rms(x, g):
    return x * lax.rsqrt(jnp.mean(x * x, axis=-1, keepdims=True) + EPS) * g


def _lane_tile(x, width):
    reps = width // LANES
    return x if reps == 1 else jnp.concatenate([x] * reps, axis=1)


def _norm_cast_kernel(x_ref, g_ref, o_ref):
    o_ref[...] = _rms(x_ref[...], g_ref[...]).astype(o_ref.dtype)


def norm_cast(x, g):
    r, d = x.shape
    tr = _tile(r, 256)
    return pl.pallas_call(
        _norm_cast_kernel,
        grid=(r // tr,),
        in_specs=[pl.BlockSpec((tr, d), lambda i: (i, 0)), pl.BlockSpec((1, d), lambda i: (0, 0))],
        out_specs=pl.BlockSpec((tr, d), lambda i: (i, 0)),
        out_shape=jax.ShapeDtypeStruct((r, d), BF16),
        compiler_params=_params(1),
        name="norm_cast",
    )(x, g.reshape(1, d))


def _resid_norm_kernel(h_ref, m_ref, gp_ref, gn_ref, ho_ref, xn_ref):
    h = h_ref[...] + _rms(m_ref[...], gp_ref[...])
    ho_ref[...] = h
    xn_ref[...] = _rms(h, gn_ref[...]).astype(xn_ref.dtype)


def _resid_kernel(h_ref, m_ref, gp_ref, ho_ref):
    ho_ref[...] = h_ref[...] + _rms(m_ref[...], gp_ref[...])


def resid_norm(h, m, g_post, g_next):
    r, d = h.shape
    tr = _tile(r, 256)
    row = pl.BlockSpec((tr, d), lambda i: (i, 0))
    vec = pl.BlockSpec((1, d), lambda i: (0, 0))
    if g_next is None:
        return pl.pallas_call(
            _resid_kernel, grid=(r // tr,), in_specs=[row, row, vec], out_specs=row,
            out_shape=jax.ShapeDtypeStruct((r, d), F32), compiler_params=_params(1), name="resid",
        )(h, m, g_post.reshape(1, d)), None
    return pl.pallas_call(
        _resid_norm_kernel, grid=(r // tr,), in_specs=[row, row, vec, vec], out_specs=[row, row],
        out_shape=[jax.ShapeDtypeStruct((r, d), F32), jax.ShapeDtypeStruct((r, d), BF16)],
        compiler_params=_params(1), name="resid_norm",
    )(h, m, g_post.reshape(1, d), g_next.reshape(1, d))


def _matmul_kernel(*refs):
    *x_refs, w_ref, o_ref = refs
    acc, k0 = None, 0
    for x_ref in x_refs:
        k1 = k0 + x_ref.shape[1]
        part = jnp.dot(x_ref[...].astype(BF16), w_ref[0, k0:k1, :], preferred_element_type=F32)
        acc = part if acc is None else acc + part
        k0 = k1
    o_ref[...] = acc.astype(o_ref.dtype)


def matmul(xs, w, layer, out_dtype, tm=1024, tn=512):
    m = xs[0].shape[0]
    k, n = w.shape[1:]
    assert sum(x.shape[1] for x in xs) == k
    tm, tn = _tile(m, tm), _tile(n, tn)
    return pl.pallas_call(
        _matmul_kernel,
        grid=(m // tm, n // tn),
        in_specs=([pl.BlockSpec((tm, x.shape[1]), lambda i, j: (i, 0)) for x in xs]
                  + [pl.BlockSpec((1, k, tn), lambda i, j: (layer, 0, j))]),
        out_specs=pl.BlockSpec((tm, tn), lambda i, j: (i, j)),
        out_shape=jax.ShapeDtypeStruct((m, n), out_dtype),
        compiler_params=_params(2),
        name="matmul",
    )(*xs, w)


def _matmul_rope_kernel(types_ref, x_ref, w_ref, cos_ref, sin_ref, o_ref, *, heads_per_block):
    j = pl.program_id(1)
    acc = jnp.dot(x_ref[...].astype(BF16), w_ref[0], preferred_element_type=F32)
    for hh in range(heads_per_block):
        t = types_ref[j * heads_per_block + hh]
        a = acc[:, hh * LANES:(hh + 1) * LANES]
        roped = a * cos_ref[t] + pltpu.roll(a, LANES // 2, 1) * sin_ref[t]
        o_ref[:, hh * LANES:(hh + 1) * LANES] = roped.astype(o_ref.dtype)


def matmul_rope(x, w, layer, head_types, cos_tab, sin_tab, out_dtype, tm=1024, tn=512):
    m, k = x.shape
    n = w.shape[2]
    n_types, period, _ = cos_tab.shape
    tm, tn = _tile(m, tm), _tile(n, tn)
    tm = _tile(period, tm)
    assert m % tm == 0
    pblocks = period // tm
    tab_spec = pl.BlockSpec((n_types, tm, LANES), lambda i, j, t: (0, i % pblocks, 0))
    grid_spec = pltpu.PrefetchScalarGridSpec(
        num_scalar_prefetch=1,
        grid=(m // tm, n // tn),
        in_specs=[pl.BlockSpec((tm, k), lambda i, j, t: (i, 0)),
                  pl.BlockSpec((1, k, tn), lambda i, j, t: (layer, 0, j)), tab_spec, tab_spec],
        out_specs=pl.BlockSpec((tm, tn), lambda i, j, t: (i, j)),
    )
    return pl.pallas_call(
        functools.partial(_matmul_rope_kernel, heads_per_block=tn // LANES),
        grid_spec=grid_spec,
        out_shape=jax.ShapeDtypeStruct((m, n), out_dtype),
        compiler_params=_params(2),
        name="matmul_rope",
    )(head_types, x, w, cos_tab, sin_tab)


def _mlp_step(x_ref, wu, wd, o_ref, first):
    @pl.when(first)
    def _():
        o_ref[...] = jnp.zeros_like(o_ref)

    h = jnp.dot(x_ref[...], wu, preferred_element_type=F32)
    h = jnp.square(jnp.maximum(h, 0.0)).astype(BF16)
    o_ref[...] += jnp.dot(h, wd, preferred_element_type=F32)


def _mlp_kernel(x_ref, wu_ref, wd_ref, o_ref):
    _mlp_step(x_ref, wu_ref[...], wd_ref[...], o_ref, pl.program_id(1) == 0)


def _mlp_cast_kernel(x_ref, wu_ref, wd_ref, o_ref, wub_ref, wdb_ref):
    wub_ref[...] = wu_ref[0].astype(BF16)
    wdb_ref[...] = wd_ref[0].astype(BF16)
    _mlp_step(x_ref, wub_ref[...], wdb_ref[...], o_ref, pl.program_id(0) == 0)


def mlp_cast(xn, w_up, w_down, layer, tf=256):
    m, d = xn.shape
    ff = w_up.shape[2]
    tf = _tile(ff, tf)
    return pl.pallas_call(
        _mlp_cast_kernel,
        grid=(ff // tf,),
        in_specs=[pl.BlockSpec((m, d), lambda f: (0, 0)), pl.BlockSpec((1, d, tf), lambda f: (layer, 0, f)),
                  pl.BlockSpec((1, tf, d), lambda f: (layer, f, 0))],
        out_specs=[pl.BlockSpec((m, d), lambda f: (0, 0)), pl.BlockSpec((d, tf), lambda f: (0, f)),
                   pl.BlockSpec((tf, d), lambda f: (f, 0))],
        out_shape=[jax.ShapeDtypeStruct((m, d), F32), jax.ShapeDtypeStruct((d, ff), BF16),
                   jax.ShapeDtypeStruct((ff, d), BF16)],
        compiler_params=_params(1),
        name="mlp_cast",
    )(xn, w_up, w_down)


def mlp(xn, w_up, w_down, tm=512, tf=512):
    m, d = xn.shape
    ff = w_up.shape[1]
    tm, tf = _tile(m, tm), _tile(ff, tf)
    return pl.pallas_call(
        _mlp_kernel,
        grid=(m // tm, ff // tf),
        in_specs=[pl.BlockSpec((tm, d), lambda i, f: (i, 0)), pl.BlockSpec((d, tf), lambda i, f: (0, f)),
                  pl.BlockSpec((tf, d), lambda i, f: (f, 0))],
        out_specs=pl.BlockSpec((tm, d), lambda i, f: (i, 0)),
        out_shape=jax.ShapeDtypeStruct((m, d), F32),
        compiler_params=_params(2),
        name="mlp",
    )(xn, w_up, w_down)


def _mla_down_kernel(x_ref, w_ref, qg_ref, kvg_ref, cos_ref, sin_ref, cq_ref, ckv_ref, kpe_ref):
    acc = jnp.dot(x_ref[...], w_ref[...], preferred_element_type=F32)
    cq_ref[...] = _rms(acc[:, :Q_LORA], qg_ref[...]).astype(cq_ref.dtype)
    ckv_ref[...] = _rms(acc[:, Q_LORA:Q_LORA + KV_LORA], kvg_ref[...])
    p = acc[:, Q_LORA + KV_LORA:]
    kpe_ref[...] = p * cos_ref[...] + pltpu.roll(p, LANES // 2, 1) * sin_ref[...]


def mla_down(xn, w_cat, q_gain, kv_gain, cos_tab, sin_tab, tm=512):
    m, d = xn.shape
    n = w_cat.shape[1]
    period = cos_tab.shape[0]
    tm = _tile(period, _tile(m, tm))
    pblocks = period // tm
    tab_spec = pl.BlockSpec((tm, LANES), lambda i: (i % pblocks, 0))
    return pl.pallas_call(
        _mla_down_kernel,
        grid=(m // tm,),
        in_specs=[pl.BlockSpec((tm, d), lambda i: (i, 0)), pl.BlockSpec((d, n), lambda i: (0, 0)),
                  pl.BlockSpec((1, Q_LORA), lambda i: (0, 0)), pl.BlockSpec((1, KV_LORA), lambda i: (0, 0)),
                  tab_spec, tab_spec],
        out_specs=[pl.BlockSpec((tm, Q_LORA), lambda i: (i, 0)), pl.BlockSpec((tm, KV_LORA), lambda i: (i, 0)),
                   pl.BlockSpec((tm, LANES), lambda i: (i, 0))],
        out_shape=[jax.ShapeDtypeStruct((m, Q_LORA), BF16), jax.ShapeDtypeStruct((m, KV_LORA), F32),
                   jax.ShapeDtypeStruct((m, LANES), F32)],
        compiler_params=_params(1),
        name="mla_down",
    )(xn, w_cat, q_gain.reshape(1, Q_LORA), kv_gain.reshape(1, KV_LORA), cos_tab, sin_tab)


def _online_softmax_step(s, v, m_sc, l_sc, acc_sc):
    tk = s.shape[1]
    m_prev = m_sc[...]
    m_new = jnp.maximum(m_prev, jnp.max(s, axis=-1, keepdims=True))
    alpha = jnp.exp2(m_prev - m_new)
    p = jnp.exp2(s - _lane_tile(m_new, tk))
    l_sc[...] = alpha * l_sc[...] + jnp.sum(p, axis=-1, keepdims=True)
    pv = jnp.dot(p.astype(BF16), v, preferred_element_type=F32)
    acc_sc[...] = acc_sc[...] * _lane_tile(alpha, pv.shape[1]) + pv
    m_sc[...] = m_new


def _band_attn_kernel(sink_ref, q_ref, k_ref, v_ref, bias_ref, o_ref, m_sc, l_sc, acc_sc, *,
                      group, tq, tk, n_delta, has_sink):
    hk = pl.program_id(1)
    qb = pl.program_id(2)
    q = jnp.concatenate([q_ref[:, g * HEAD_DIM:(g + 1) * HEAD_DIM] for g in range(group)], axis=0)
    q = (q * (ATTN_SCALE * LOG2E)).astype(BF16)

    if has_sink:
        for g in range(group):
            m_sc[g * tq:(g + 1) * tq, :] = jnp.full((tq, LANES), sink_ref[hk * group + g], F32)
        l_sc[...] = jnp.ones_like(l_sc)
    else:
        m_sc[...] = jnp.full_like(m_sc, NEG_BIG)
        l_sc[...] = jnp.zeros_like(l_sc)
    acc_sc[...] = jnp.zeros_like(acc_sc)

    def body(d, carry):
        start = pl.multiple_of((qb - d) * tk, tk)
        k = k_ref[pl.ds(start, tk), :].astype(BF16)
        v = v_ref[pl.ds(start, tk), :].astype(BF16)
        s = lax.dot_general(q, k, (((1,), (1,)), ((), ())), preferred_element_type=F32)
        s = (s.reshape(group, tq, tk) + bias_ref[d][None]).reshape(group * tq, tk)
        _online_softmax_step(s, v, m_sc, l_sc, acc_sc)
        return carry

    lax.fori_loop(0, jnp.minimum(qb + 1, n_delta), body, 0)
    out = acc_sc[...] / l_sc[...]
    for g in range(group):
        o_ref[:, g * HEAD_DIM:(g + 1) * HEAD_DIM] = out[g * tq:(g + 1) * tq, :].astype(o_ref.dtype)


def _band_bias(tq, tk, n_delta, patterns):
    i = np.arange(tq)[None, :, None]
    j = np.arange(tk)[None, None, :]
    dist = np.arange(n_delta)[:, None, None] * tk + i - j
    count = np.zeros(dist.shape, np.float64)
    for (w, d) in patterns:
        count += (dist >= 0) & (dist <= w) & (dist % d == 0)
    with np.errstate(divide="ignore"):
        return np.log2(count).astype(np.float32)


def band_attention(proj, seq, q_col, k_col, v_col, n_kv, group, patterns, sink):
    rows = proj.shape[0]
    batch = rows // seq
    tq = tk = _tile(seq, 256)
    nq = seq // tq
    reach = max(w for (w, _) in patterns)
    n_delta = min(nq, -(-reach // tk) + 1)
    bias = jnp.asarray(_band_bias(tq, tk, n_delta, patterns))
    has_sink = sink is not None
    if sink is None:
        sink = jnp.zeros((n_kv * group,), F32)
    gw = group * HEAD_DIM
    grid_spec = pltpu.PrefetchScalarGridSpec(
        num_scalar_prefetch=0,
        grid=(batch, n_kv, nq),
        in_specs=[
            pl.BlockSpec(memory_space=pltpu.SMEM),
            pl.BlockSpec((tq, gw), lambda b, h, i: (b * nq + i, q_col + h)),
            pl.BlockSpec((seq, HEAD_DIM), lambda b, h, i: (b, k_col + h)),
            pl.BlockSpec((seq, HEAD_DIM), lambda b, h, i: (b, v_col + h)),
            pl.BlockSpec((n_delta, tq, tk), lambda b, h, i: (0, 0, 0)),
        ],
        out_specs=pl.BlockSpec((tq, gw), lambda b, h, i: (b * nq + i, h)),
        scratch_shapes=[pltpu.VMEM((group * tq, LANES), F32), pltpu.VMEM((group * tq, LANES), F32),
                        pltpu.VMEM((group * tq, HEAD_DIM), F32)],
    )
    return pl.pallas_call(
        functools.partial(_band_attn_kernel, group=group, tq=tq, tk=tk, n_delta=n_delta, has_sink=has_sink),
        grid_spec=grid_spec,
        out_shape=jax.ShapeDtypeStruct((rows, n_kv * gw), BF16),
        compiler_params=_params(3),
        name="band_attention",
    )(sink.astype(F32) * LOG2E, proj, proj, proj, bias)


def _mla_prompt_kernel(q_ref, kn_ref, v_ref, kpe_ref, o_ref, m_sc, l_sc, acc_sc, *, tq, tk):
    qb = pl.program_id(2)
    q = q_ref[...]
    m_sc[...] = jnp.full_like(m_sc, NEG_BIG)
    l_sc[...] = jnp.zeros_like(l_sc)
    acc_sc[...] = jnp.zeros_like(acc_sc)

    def scores(kb):
        start = pl.multiple_of(kb * tk, tk)
        k = jnp.concatenate([kn_ref[pl.ds(start, tk), :], kpe_ref[pl.ds(start, tk), :]], axis=1)
        s = lax.dot_general(q, k, (((1,), (1,)), ((), ())), preferred_element_type=F32)
        return s, v_ref[pl.ds(start, tk), :]

    def body(kb, carry):
        s, v = scores(kb)
        _online_softmax_step(s, v, m_sc, l_sc, acc_sc)
        return carry

    lax.fori_loop(0, qb, body, 0)
    s, v = scores(qb)
    causal = lax.broadcasted_iota(jnp.int32, (tq, tk), 0) >= lax.broadcasted_iota(jnp.int32, (tq, tk), 1)
    _online_softmax_step(jnp.where(causal, s, -jnp.inf), v, m_sc, l_sc, acc_sc)
    o_ref[...] = (acc_sc[...] / l_sc[...]).astype(o_ref.dtype)


def mla_prompt_attention(q_full, kv, kpe, seq):
    rows = q_full.shape[0]
    batch = rows // seq
    tq = tk = _tile(seq, 512)
    nq = seq // tq
    return pl.pallas_call(
        functools.partial(_mla_prompt_kernel, tq=tq, tk=tk),
        grid=(batch, H_C, nq),
        in_specs=[
            pl.BlockSpec((tq, 2 * LANES), lambda b, h, i: (b * nq + i, h)),
            pl.BlockSpec((seq, LANES), lambda b, h, i: (b, 2 * h)),
            pl.BlockSpec((seq, LANES), lambda b, h, i: (b, 2 * h + 1)),
            pl.BlockSpec((seq, LANES), lambda b, h, i: (b, 0)),
        ],
        out_specs=pl.BlockSpec((tq, V_DIM), lambda b, h, i: (b * nq + i, h)),
        out_shape=jax.ShapeDtypeStruct((rows, H_C * V_DIM), BF16),
        scratch_shapes=[pltpu.VMEM((tq, LANES), F32), pltpu.VMEM((tq, LANES), F32), pltpu.VMEM((tq, V_DIM), F32)],
        compiler_params=_params(3),
        name="mla_prompt_attention",
    )(q_full, kv, kv, kpe)


def _sample_win_kernel(q_ref, kc_ref, vc_ref, kn_ref, vn_ref, bc_ref, bn_ref, m0_ref, l0_ref, o_ref):
    q = q_ref[0]
    s_c = jnp.dot(kc_ref[0, 0].astype(BF16), q, preferred_element_type=F32) + bc_ref[...]
    s_n = jnp.dot(kn_ref[0].astype(BF16), q, preferred_element_type=F32) + bn_ref[...]
    m0 = m0_ref[...]
    m = jnp.maximum(jnp.maximum(jnp.max(s_c, axis=0, keepdims=True), jnp.max(s_n, axis=0, keepdims=True)), m0)
    p_c = jnp.exp2(s_c - m)
    p_n = jnp.exp2(s_n - m)
    den = (jnp.sum(p_c, axis=0, keepdims=True) + jnp.sum(p_n, axis=0, keepdims=True)
           + l0_ref[...] * jnp.exp2(m0 - m))
    contract_keys = (((0,), (0,)), ((), ()))
    o = (lax.dot_general(p_c.astype(BF16), vc_ref[0, 0].astype(BF16), contract_keys, preferred_element_type=F32)
         + lax.dot_general(p_n.astype(BF16), vn_ref[0].astype(BF16), contract_keys, preferred_element_type=F32))
    o_ref[0] = o / jnp.broadcast_to(den, (LANES, LANES)).T


def _sample_bias(n_cache, n_new_pad, steps, group, n_kv, patterns):
    cols = np.arange(LANES)
    col_kv = cols // (steps * group)
    t = (cols % (steps * group)) // group
    live = cols < n_kv * steps * group

    def table(pos, n_valid):
        row_pos = np.repeat(pos, n_kv)
        row_kv = np.tile(np.arange(n_kv), len(pos))
        dist = (n_cache + t)[None, :] - row_pos[:, None]
        count = np.zeros(dist.shape, np.float64)
        for (w, d) in patterns:
            count += (dist >= 0) & (dist <= w) & (dist % d == 0)
        count = np.where(row_kv[:, None] == col_kv[None, :], count, 0.0)
        with np.errstate(divide="ignore"):
            b = np.where(live[None, :], np.log2(count), 0.0)
        b[n_valid * n_kv:, :] = -np.inf
        return b.astype(np.float32)

    return (table(np.arange(n_cache), n_cache),
            table(n_cache + np.arange(n_new_pad // n_kv), steps))


def sample_window_attention(q, k_new, v_new, k_cache, v_cache, layer, n_kv, group, patterns, sink):
    db, steps, hq, _ = q.shape
    n_layers, _, n_cache = k_cache.shape[:3]
    n_cols = n_kv * steps * group
    new_pad = 16
    qcat = (q * (ATTN_SCALE * LOG2E)).reshape(db, steps, n_kv, group, HEAD_DIM).transpose(0, 4, 2, 1, 3)
    qcat = jnp.pad(qcat.reshape(db, HEAD_DIM, n_cols), ((0, 0), (0, 0), (0, LANES - n_cols))).astype(BF16)
    rows = lambda x: x.reshape(n_layers, db, n_cache * n_kv, HEAD_DIM)
    pad_new = lambda x: jnp.pad(x.reshape(db, steps * n_kv, HEAD_DIM), ((0, 0), (0, new_pad - steps * n_kv), (0, 0)))
    bias_c, bias_n = _sample_bias(n_cache, new_pad, steps, group, n_kv, patterns)
    if sink is None:
        m0 = jnp.full((1, LANES), NEG_BIG, F32)
        l0 = jnp.zeros((1, LANES), F32)
    else:
        per_col = jnp.broadcast_to(sink.reshape(n_kv, 1, group), (n_kv, steps, group)).reshape(n_cols)
        m0 = jnp.pad(per_col.astype(F32) * LOG2E, (0, LANES - n_cols)).reshape(1, LANES)
        l0 = jnp.ones((1, LANES), F32)
    cache_blk = pl.BlockSpec((1, 1, n_cache * n_kv, HEAD_DIM), lambda b: (layer, b, 0, 0))
    new_blk = pl.BlockSpec((1, new_pad, HEAD_DIM), lambda b: (b, 0, 0))
    const = lambda shape: pl.BlockSpec(shape, lambda b: (0,) * len(shape))
    out = pl.pallas_call(
        _sample_win_kernel,
        grid=(db,),
        in_specs=[pl.BlockSpec((1, HEAD_DIM, LANES), lambda b: (b, 0, 0)), cache_blk, cache_blk, new_blk, new_blk,
                  const((n_cache * n_kv, LANES)), const((new_pad, LANES)), const((1, LANES)), const((1, LANES))],
        out_specs=pl.BlockSpec((1, LANES, HEAD_DIM), lambda b: (b, 0, 0)),
        out_shape=jax.ShapeDtypeStruct((db, LANES, HEAD_DIM), F32),
        compiler_params=_params(1),
        name="sample_window_attention",
    )(qcat, rows(k_cache), rows(v_cache), pad_new(k_new), pad_new(v_new),
      jnp.asarray(bias_c), jnp.asarray(bias_n), m0, l0)
    out = out[:, :n_cols].reshape(db, n_kv, steps, group, HEAD_DIM).transpose(0, 2, 1, 3, 4)
    return out.reshape(db * steps, hq * HEAD_DIM)


def _head_matmul_kernel(x_ref, w_ref, o_ref, *, contract_w_cols):
    dims = (((1,), (1 if contract_w_cols else 0,)), ((), ()))
    o_ref[...] = lax.dot_general(x_ref[...], w_ref[0], dims, preferred_element_type=F32).astype(o_ref.dtype)


def head_matmul(x, w, layer, x_width, x_stride, w_off, out_width, contract_w_cols):
    m = x.shape[0]
    return pl.pallas_call(
        functools.partial(_head_matmul_kernel, contract_w_cols=contract_w_cols),
        grid=(H_C,),
        in_specs=[pl.BlockSpec((m, x_width), lambda h: (0, h * x_stride)),
                  pl.BlockSpec((1, KV_LORA, LANES), lambda h: (layer, 0, 2 * h + w_off))],
        out_specs=pl.BlockSpec((m, out_width), lambda h: (0, h)),
        out_shape=jax.ShapeDtypeStruct((m, H_C * out_width), BF16),
        compiler_params=_params(1),
        name="head_matmul",
    )(x, w)


def _mla_sample_kernel(*refs, pages_per_step, page, steps):
    pt_ref, ql_ref, qp_ref, cn_ref, kn_ref = refs[:5]
    ckv_refs = refs[5:5 + pages_per_step]
    kpe_refs = refs[5 + pages_per_step:5 + 2 * pages_per_step]
    o_ref, m_sc, l_sc, acc_sc, ckv_sc, kpe_sc = refs[5 + 2 * pages_per_step:]
    del pt_ref
    c = pl.program_id(1)
    contract_rows = (((0,), (0,)), ((), ()))
    n_q = acc_sc.shape[0]

    @pl.when(c == 0)
    def _():
        m_sc[...] = jnp.full_like(m_sc, NEG_BIG)
        l_sc[...] = jnp.zeros_like(l_sc)
        acc_sc[...] = jnp.zeros_like(acc_sc)

    for k in range(pages_per_step):
        ckv_sc[k * page:(k + 1) * page, :] = ckv_refs[k][0, 0].astype(BF16)
        kpe_sc[:, k * page:(k + 1) * page] = kpe_refs[k][0, 0].astype(BF16)
    ql = ql_ref[0]
    qp = qp_ref[0]

    def per_query_rows(stat):
        return _lane_tile(jnp.broadcast_to(stat, (n_q, n_q)).T, KV_LORA)

    def update(s, values):
        m_prev = m_sc[...]
        m_new = jnp.maximum(m_prev, jnp.max(s, axis=0, keepdims=True))
        alpha = jnp.exp2(m_prev - m_new)
        p = jnp.exp2(s - m_new)
        l_sc[...] = alpha * l_sc[...] + jnp.sum(p, axis=0, keepdims=True)
        pv = lax.dot_general(p.astype(BF16), values, contract_rows, preferred_element_type=F32)
        acc_sc[...] = acc_sc[...] * per_query_rows(alpha) + pv
        m_sc[...] = m_new

    ckv = ckv_sc[...]
    s = (jnp.dot(ckv, ql, preferred_element_type=F32)
         + lax.dot_general(kpe_sc[...], qp, contract_rows, preferred_element_type=F32))
    update(s, ckv)

    @pl.when(c == pl.num_programs(1) - 1)
    def _():
        cn = cn_ref[0].astype(BF16)
        s_n = (jnp.dot(cn, ql, preferred_element_type=F32)
               + jnp.dot(kn_ref[0].astype(BF16), qp, preferred_element_type=F32))
        t_key = lax.broadcasted_iota(jnp.int32, s_n.shape, 0)
        t_query = lax.broadcasted_iota(jnp.int32, s_n.shape, 1) // (n_q // steps)
        update(jnp.where((t_key <= t_query) & (t_key < steps), s_n, -jnp.inf), cn)
        o_ref[0] = (acc_sc[...] / per_query_rows(l_sc[...])).astype(o_ref.dtype)


def mla_sample_attention(q_lat, q_pe, ckv_new, kpe_new, pool_ckv, pool_kpe_t, layer, page_table, steps):
    db, _, rows = q_lat.shape
    n_pages = page_table.shape[1]
    page = pool_ckv.shape[2]
    pps = min(16, n_pages)
    assert n_pages % pps == 0 and rows == LANES
    new_pad = 16
    pad_new = lambda x: jnp.pad(x, ((0, 0), (0, new_pad - steps), (0, 0)))

    def page_spec(shape, slot):
        return pl.BlockSpec((1, 1) + shape, lambda b, c, pt: (layer, pt[b, c * pps + slot], 0, 0))

    per_seq = lambda n, w: pl.BlockSpec((1, n, w), lambda b, c, pt: (b, 0, 0))
    grid_spec = pltpu.PrefetchScalarGridSpec(
        num_scalar_prefetch=1,
        grid=(db, n_pages // pps),
        in_specs=([per_seq(KV_LORA, rows), per_seq(QK_ROPE, rows), per_seq(new_pad, KV_LORA), per_seq(new_pad, QK_ROPE)]
                  + [page_spec((page, KV_LORA), k) for k in range(pps)]
                  + [page_spec((QK_ROPE, page), k) for k in range(pps)]),
        out_specs=per_seq(rows, KV_LORA),
        scratch_shapes=[pltpu.VMEM((1, rows), F32), pltpu.VMEM((1, rows), F32),
                        pltpu.VMEM((rows, KV_LORA), F32), pltpu.VMEM((pps * page, KV_LORA), BF16),
                        pltpu.VMEM((QK_ROPE, pps * page), BF16)],
    )
    return pl.pallas_call(
        functools.partial(_mla_sample_kernel, pages_per_step=pps, page=page, steps=steps),
        grid_spec=grid_spec,
        out_shape=jax.ShapeDtypeStruct((db, rows, KV_LORA), BF16),
        compiler_params=_params(2),
        name="mla_sample_attention",
    )(page_table, q_lat, q_pe, pad_new(ckv_new), pad_new(kpe_new), *([pool_ckv] * pps), *([pool_kpe_t] * pps))


def _rope_angles(pos, dim):
    inv = ROPE_THETA ** (-jnp.arange(0, dim, 2, dtype=F32) / dim)
    ang = pos.astype(F32)[:, None] * inv[None, :]
    return jnp.cos(ang), jnp.sin(ang)


def _head_rope_tables(pos):
    c, s = _rope_angles(pos, HEAD_DIM)
    cos = jnp.stack([jnp.concatenate([c, c], 1), jnp.ones((pos.shape[0], LANES), F32)])
    sin = jnp.stack([jnp.concatenate([-s, s], 1), jnp.zeros((pos.shape[0], LANES), F32)])
    return cos, sin


def _mla_rope_tables(pos, scale=1.0):
    c, s = _rope_angles(pos, QK_ROPE)
    z = jnp.zeros_like(c)
    cos = jnp.stack([jnp.concatenate([c, z, c, z], 1), jnp.ones((pos.shape[0], LANES), F32)])
    sin = jnp.stack([jnp.concatenate([-s, z, s, z], 1), jnp.zeros((pos.shape[0], LANES), F32)])
    return cos * scale, sin * scale


def _spread_rope_cols(w):
    half = QK_ROPE // 2
    z = jnp.zeros(w.shape[:-1] + (half,), w.dtype)
    return jnp.concatenate([w[..., :half], z, w[..., half:], z], axis=-1)


def _gather_rope_cols(x):
    half = QK_ROPE // 2
    return jnp.concatenate([x[..., :half], x[..., 2 * half:3 * half]], axis=-1)


EVEN_HEAD_TYPES = np.array([0] * (HQ_A + HKV_A) + [1] * HKV_A + [0] * (HQ_B + HKV_B) + [1] * HKV_B, np.int32)
MLA_Q_HEAD_TYPES = np.array([1, 0] * H_C, np.int32)
COL_QA, COL_KA, COL_VA = 0, HQ_A, HQ_A + HKV_A
COL_QB, COL_KB, COL_VB = HQ_A + 2 * HKV_A, HQ_A + 2 * HKV_A + HQ_B, HQ_A + 2 * HKV_A + HQ_B + HKV_B


def _cols(x, first_head, n_heads):
    return x[:, first_head * HEAD_DIM:(first_head + n_heads) * HEAD_DIM]


def kernel(x_prompt, x_sample, cache_a_k, cache_a_v, cache_b_k, cache_b_v, cache_mla_ckv, cache_mla_kpe,
           page_table, norm_mix_pre, norm_mix_post, norm_ffn_pre, norm_ffn_post, w_in_e, sink_b, w_out_e,
           w_dq, q_norm, w_uq, w_dkv, kv_norm, w_ukv, w_o_mla, w_up, w_down):
    batch, seq, d_model = x_prompt.shape
    db, steps, _ = x_sample.shape
    depth = norm_mix_pre.shape[0]
    past_len = page_table.shape[1] * cache_mla_ckv.shape[2]
    keep_a = min(max(w for (w, _) in A_PATTERNS), seq)
    keep_b = min(WIN_B, seq)

    pos_p = jnp.arange(seq)
    pos_s = jnp.tile(past_len + jnp.arange(steps), db)
    rope_p, rope_s = _head_rope_tables(pos_p), _head_rope_tables(pos_s)
    mrope_p, mrope_s = _mla_rope_tables(pos_p), _mla_rope_tables(pos_s)
    mq_p, mq_s = _mla_rope_tables(pos_p, MLA_SCALE * LOG2E), _mla_rope_tables(pos_s, MLA_SCALE * LOG2E)
    even_types = jnp.asarray(EVEN_HEAD_TYPES)
    mla_q_types = jnp.asarray(MLA_Q_HEAD_TYPES)
    w_in_b, w_out_b = w_in_e.astype(BF16), w_out_e.astype(BF16)
    w_kv_b, w_o_b = w_ukv.astype(BF16), w_o_mla.astype(BF16)
    pool_kpe_t = jnp.swapaxes(cache_mla_kpe, 2, 3)

    hp = x_prompt.reshape(batch * seq, d_model)
    hs = x_sample.reshape(db * steps, d_model)
    xp = norm_cast(hp, norm_mix_pre[0])
    xs = norm_cast(hs, norm_mix_pre[0])
    outs = {name: [] for name in ("p_ak", "p_av", "p_bk", "p_bv", "p_ckv", "p_kpe",
                                  "s_ak", "s_av", "s_bk", "s_bv", "s_ckv", "s_kpe")}

    for layer in range(depth):
        li = layer // 2
        if layer % 2 == 0:
            proj = matmul_rope(xp, w_in_b, li, even_types, *rope_p, F32)
            oa = band_attention(proj, seq, COL_QA // (HQ_A // HKV_A), COL_KA, COL_VA, HKV_A, HQ_A // HKV_A,
                                A_PATTERNS, None)
            ob = band_attention(proj, seq, COL_QB // (HQ_B // HKV_B), COL_KB, COL_VB, HKV_B, HQ_B // HKV_B,
                                ((WIN_B, 1),), sink_b[li])
            mix_p = matmul([oa, ob], w_out_b, li, F32)
            per_seq = lambda x, n: x.reshape(batch, seq, n, HEAD_DIM)
            outs["p_ak"].append(per_seq(_cols(proj, COL_KA, HKV_A), HKV_A)[:, seq - keep_a:])
            outs["p_av"].append(per_seq(_cols(proj, COL_VA, HKV_A), HKV_A)[:, seq - keep_a:])
            outs["p_bk"].append(per_seq(_cols(proj, COL_KB, HKV_B), HKV_B)[:, seq - keep_b:])
            outs["p_bv"].append(per_seq(_cols(proj, COL_VB, HKV_B), HKV_B)[:, seq - keep_b:])
            proj = matmul_rope(xs, w_in_b, li, even_types, *rope_s, F32)
            per_tok = lambda x, n: x.reshape(db, steps, n, HEAD_DIM)
            ka, va = per_tok(_cols(proj, COL_KA, HKV_A), HKV_A), per_tok(_cols(proj, COL_VA, HKV_A), HKV_A)
            kb, vb = per_tok(_cols(proj, COL_KB, HKV_B), HKV_B), per_tok(_cols(proj, COL_VB, HKV_B), HKV_B)
            oa = sample_window_attention(per_tok(_cols(proj, COL_QA, HQ_A), HQ_A), ka, va, cache_a_k, cache_a_v,
                                         li, HKV_A, HQ_A // HKV_A, A_PATTERNS, None)
            ob = sample_window_attention(per_tok(_cols(proj, COL_QB, HQ_B), HQ_B), kb, vb, cache_b_k, cache_b_v,
                                         li, HKV_B, HQ_B // HKV_B, ((WIN_B, 1),), sink_b[li])
            mix_s = matmul([oa, ob], w_out_b, li, F32)
            outs["s_ak"].append(ka); outs["s_av"].append(va); outs["s_bk"].append(kb); outs["s_bv"].append(vb)
        else:
            w_cat = jnp.concatenate([w_dq[li], w_dkv[li][:, :KV_LORA], _spread_rope_cols(w_dkv[li][:, KV_LORA:])],
                                    axis=1).astype(BF16)
            w_uq_l = w_uq[li].reshape(Q_LORA, H_C, QK_NOPE + QK_ROPE)
            w_q = jnp.concatenate([w_uq_l[..., :QK_NOPE], _spread_rope_cols(w_uq_l[..., QK_NOPE:])], axis=-1)
            w_q = w_q.reshape(1, Q_LORA, H_C * 2 * LANES).astype(BF16)
            cq, ckv, kpe = mla_down(xp, w_cat, q_norm[li], kv_norm[li], mrope_p[0][0], mrope_p[1][0])
            q_full = matmul_rope(cq, w_q, 0, mla_q_types, *mq_p, BF16, tn=1024)
            kv = matmul([ckv], w_kv_b, li, BF16, tn=1024)
            o = mla_prompt_attention(q_full, kv, kpe.astype(BF16), seq)
            mix_p = matmul([o], w_o_b, li, F32)
            outs["p_ckv"].append(ckv.reshape(batch, seq, KV_LORA))
            outs["p_kpe"].append(_gather_rope_cols(kpe).reshape(batch, seq, QK_ROPE))
            cq, ckv, kpe = mla_down(xs, w_cat, q_norm[li], kv_norm[li], mrope_s[0][0], mrope_s[1][0])
            q_full = matmul_rope(cq, w_q, 0, mla_q_types, *mq_s, BF16, tn=1024)
            q_lat = head_matmul(q_full, w_kv_b, li, LANES, 2, 0, KV_LORA, True)
            q_pe = _gather_rope_cols(q_full.reshape(db * steps, H_C, 2, LANES)[:, :, 1, :])
            ckv_s = ckv.reshape(db, steps, KV_LORA)
            kpe_s = _gather_rope_cols(kpe).reshape(db, steps, QK_ROPE)
            queries_as_cols = lambda x, w: x.reshape(db, steps * H_C, w).transpose(0, 2, 1)
            o_lat = mla_sample_attention(queries_as_cols(q_lat, KV_LORA), queries_as_cols(q_pe, QK_ROPE), ckv_s, kpe_s,
                                         cache_mla_ckv, pool_kpe_t, li, page_table, steps)
            o = head_matmul(o_lat.reshape(db * steps, H_C * KV_LORA), w_kv_b, li, KV_LORA, 1, 1, V_DIM, False)
            mix_s = matmul([o], w_o_b, li, F32)
            outs["s_ckv"].append(ckv_s); outs["s_kpe"].append(kpe_s)

        hp, xp = resid_norm(hp, mix_p, norm_mix_post[layer], norm_ffn_pre[layer])
        hs, xs = resid_norm(hs, mix_s, norm_mix_post[layer], norm_ffn_pre[layer])
        g_next = norm_mix_pre[layer + 1] if layer + 1 < depth else None
        mlp_s, w_u, w_d = mlp_cast(xs, w_up, w_down, layer)
        hs, xs = resid_norm(hs, mlp_s, norm_ffn_post[layer], g_next)
        hp, xp = resid_norm(hp, mlp(xp, w_u, w_d), norm_ffn_post[layer], g_next)

    stack = lambda name: jnp.stack(outs[name])
    return (hp.reshape(batch, seq, d_model), hs.reshape(db, steps, d_model),
            stack("p_ak"), stack("p_av"), stack("p_bk"), stack("p_bv"), stack("p_ckv"), stack("p_kpe"),
            stack("s_ak"), stack("s_av"), stack("s_bk"), stack("s_bv"), stack("s_ckv"), stack("s_kpe"))
```

```python
import functools

import numpy as np
import jax
import jax.numpy as jnp
from jax import lax
from jax.experimental import pallas as pl
from jax.experimental.pallas import tpu as pltpu

F32 = jnp.float32
BF16 = jnp.bfloat16

HEAD_DIM = 128
LANES = 128
HQ_A, HKV_A = 16, 4
HQ_B, HKV_B = 16, 2
A_PATTERNS = ((128, 1), (512, 4), (2048, 16))
WIN_B = 128
H_C = 32
Q_LORA, KV_LORA = 1024, 512
QK_NOPE, QK_ROPE, V_DIM = 128, 64, 128
MLA_SCALE = (QK_NOPE + QK_ROPE) ** -0.5
ATTN_SCALE = HEAD_DIM ** -0.5
LOG2E = 1.4426950408889634
ROPE_THETA = 10000.0
EPS = 1e-6
NEG_BIG = -1e30
VMEM_LIMIT = 56 * 1024 * 1024


def _params(n_axes):
    return pltpu.CompilerParams(dimension_semantics=("arbitrary",) * n_axes, vmem_limit_bytes=VMEM_LIMIT)


def _tile(n, t):
    t = min(n, t)
    assert n % t == 0, (n, t)
    return t


def _rms(x, g):
    return x * lax.rsqrt(jnp.mean(x * x, axis=-1, keepdims=True) + EPS) * g


def _lane_tile(x, width):
    reps = width // LANES
    return x if reps == 1 else jnp.concatenate([x] * reps, axis=1)


def _norm_cast_kernel(x_ref, g_ref, o_ref):
    o_ref[...] = _rms(x_ref[...], g_ref[...]).astype(o_ref.dtype)


def norm_cast(x, g):
    r, d = x.shape
    tr = _tile(r, 256)
    return pl.pallas_call(
        _norm_cast_kernel,
        grid=(r // tr,),
        in_specs=[pl.BlockSpec((tr, d), lambda i: (i, 0)), pl.BlockSpec((1, d), lambda i: (0, 0))],
        out_specs=pl.BlockSpec((tr, d), lambda i: (i, 0)),
        out_shape=jax.ShapeDtypeStruct((r, d), BF16),
        compiler_params=_params(1),
        name="norm_cast",
    )(x, g.reshape(1, d))


def _resid_norm_kernel(h_ref, m_ref, gp_ref, gn_ref, ho_ref, xn_ref):
    h = h_ref[...] + _rms(m_ref[...], gp_ref[...])
    ho_ref[...] = h
    xn_ref[...] = _rms(h, gn_ref[...]).astype(xn_ref.dtype)


def _resid_kernel(h_ref, m_ref, gp_ref, ho_ref):
    ho_ref[...] = h_ref[...] + _rms(m_ref[...], gp_ref[...])


def resid_norm(h, m, g_post, g_next):
    r, d = h.shape
    tr = _tile(r, 256)
    row = pl.BlockSpec((tr, d), lambda i: (i, 0))
    vec = pl.BlockSpec((1, d), lambda i: (0, 0))
    if g_next is None:
        return pl.pallas_call(
            _resid_kernel, grid=(r // tr,), in_specs=[row, row, vec], out_specs=row,
            out_shape=jax.ShapeDtypeStruct((r, d), F32), compiler_params=_params(1), name="resid",
        )(h, m, g_post.reshape(1, d)), None
    return pl.pallas_call(
        _resid_norm_kernel, grid=(r // tr,), in_specs=[row, row, vec, vec], out_specs=[row, row],
        out_shape=[jax.ShapeDtypeStruct((r, d), F32), jax.ShapeDtypeStruct((r, d), BF16)],
        compiler_params=_params(1), name="resid_norm",
    )(h, m, g_post.reshape(1, d), g_next.reshape(1, d))


def _matmul_kernel(*refs):
    *x_refs, w_ref, o_ref = refs
    acc, k0 = None, 0
    for x_ref in x_refs:
        k1 = k0 + x_ref.shape[1]
        part = jnp.dot(x_ref[...].astype(BF16), w_ref[0, k0:k1, :], preferred_element_type=F32)
        acc = part if acc is None else acc + part
        k0 = k1
    o_ref[...] = acc.astype(o_ref.dtype)


def matmul(xs, w, layer, out_dtype, tm=1024, tn=512):
    m = xs[0].shape[0]
    k, n = w.shape[1:]
    assert sum(x.shape[1] for x in xs) == k
    tm, tn = _tile(m, tm), _tile(n, tn)
    return pl.pallas_call(
        _matmul_kernel,
        grid=(m // tm, n // tn),
        in_specs=([pl.BlockSpec((tm, x.shape[1]), lambda i, j: (i, 0)) for x in xs]
                  + [pl.BlockSpec((1, k, tn), lambda i, j: (layer, 0, j))]),
        out_specs=pl.BlockSpec((tm, tn), lambda i, j: (i, j)),
        out_shape=jax.ShapeDtypeStruct((m, n), out_dtype),
        compiler_params=_params(2),
        name="matmul",
    )(*xs, w)


def _matmul_rope_kernel(types_ref, x_ref, w_ref, cos_ref, sin_ref, o_ref, *, heads_per_block):
    j = pl.program_id(1)
    acc = jnp.dot(x_ref[...].astype(BF16), w_ref[0], preferred_element_type=F32)
    for hh in range(heads_per_block):
        t = types_ref[j * heads_per_block + hh]
        a = acc[:, hh * LANES:(hh + 1) * LANES]
        roped = a * cos_ref[t] + pltpu.roll(a, LANES // 2, 1) * sin_ref[t]
        o_ref[:, hh * LANES:(hh + 1) * LANES] = roped.astype(o_ref.dtype)


def matmul_rope(x, w, layer, head_types, cos_tab, sin_tab, out_dtype, tm=1024, tn=512):
    m, k = x.shape
    n = w.shape[2]
    n_types, period, _ = cos_tab.shape
    tm, tn = _tile(m, tm), _tile(n, tn)
    tm = _tile(period, tm)
    assert m % tm == 0
    pblocks = period // tm
    tab_spec = pl.BlockSpec((n_types, tm, LANES), lambda i, j, t: (0, i % pblocks, 0))
    grid_spec = pltpu.PrefetchScalarGridSpec(
        num_scalar_prefetch=1,
        grid=(m // tm, n // tn),
        in_specs=[pl.BlockSpec((tm, k), lambda i, j, t: (i, 0)),
                  pl.BlockSpec((1, k, tn), lambda i, j, t: (layer, 0, j)), tab_spec, tab_spec],
        out_specs=pl.BlockSpec((tm, tn), lambda i, j, t: (i, j)),
    )
    return pl.pallas_call(
        functools.partial(_matmul_rope_kernel, heads_per_block=tn // LANES),
        grid_spec=grid_spec,
        out_shape=jax.ShapeDtypeStruct((m, n), out_dtype),
        compiler_params=_params(2),
        name="matmul_rope",
    )(head_types, x, w, cos_tab, sin_tab)


MLP_OUT_CHUNK = 1024


def _mlp_step(x_ref, wu_ref, wd_ref, o_ref, first):
    @pl.when(first)
    def _():
        o_ref[...] = jnp.zeros_like(o_ref)

    h = jnp.dot(x_ref[...], wu_ref[...], preferred_element_type=F32)
    h = jnp.square(jnp.maximum(h, 0.0)).astype(BF16)
    d = o_ref.shape[1]
    tn = min(d, MLP_OUT_CHUNK)
    for c in range(d // tn):
        o_ref[:, c * tn:(c + 1) * tn] += jnp.dot(h, wd_ref[:, c * tn:(c + 1) * tn], preferred_element_type=F32)


def _mlp_kernel(x_ref, wu_ref, wd_ref, o_ref):
    _mlp_step(x_ref, wu_ref, wd_ref, o_ref, pl.program_id(1) == 0)


def _mlp_cast_kernel(x_ref, wu_ref, wd_ref, o_ref, wub_ref, wdb_ref):
    wub_ref[...] = wu_ref[0].astype(BF16)
    wdb_ref[...] = wd_ref[0].astype(BF16)
    _mlp_step(x_ref, wub_ref, wdb_ref, o_ref, pl.program_id(0) == 0)


def mlp_cast(xn, w_up, w_down, layer, tf=256):
    m, d = xn.shape
    ff = w_up.shape[2]
    tf = _tile(ff, tf)
    return pl.pallas_call(
        _mlp_cast_kernel,
        grid=(ff // tf,),
        in_specs=[pl.BlockSpec((m, d), lambda f: (0, 0)), pl.BlockSpec((1, d, tf), lambda f: (layer, 0, f)),
                  pl.BlockSpec((1, tf, d), lambda f: (layer, f, 0))],
        out_specs=[pl.BlockSpec((m, d), lambda f: (0, 0)), pl.BlockSpec((d, tf), lambda f: (0, f)),
                   pl.BlockSpec((tf, d), lambda f: (f, 0))],
        out_shape=[jax.ShapeDtypeStruct((m, d), F32), jax.ShapeDtypeStruct((d, ff), BF16),
                   jax.ShapeDtypeStruct((ff, d), BF16)],
        compiler_params=_params(1),
        name="mlp_cast",
    )(xn, w_up, w_down)


def mlp(xn, w_up, w_down, tm=512, tf=1024):
    m, d = xn.shape
    ff = w_up.shape[1]
    tm, tf = _tile(m, tm), _tile(ff, tf)
    once = pl.Buffered(1)
    return pl.pallas_call(
        _mlp_kernel,
        grid=(m // tm, ff // tf),
        in_specs=[pl.BlockSpec((tm, d), lambda i, f: (i, 0), pipeline_mode=once),
                  pl.BlockSpec((d, tf), lambda i, f: (0, f)), pl.BlockSpec((tf, d), lambda i, f: (f, 0))],
        out_specs=pl.BlockSpec((tm, d), lambda i, f: (i, 0), pipeline_mode=once),
        out_shape=jax.ShapeDtypeStruct((m, d), F32),
        compiler_params=_params(2),
        name="mlp",
    )(xn, w_up, w_down)


def _mla_down_kernel(x_ref, w_ref, qg_ref, kvg_ref, cos_ref, sin_ref, cq_ref, ckv_ref, kpe_ref):
    acc = jnp.dot(x_ref[...], w_ref[...], preferred_element_type=F32)
    cq_ref[...] = _rms(acc[:, :Q_LORA], qg_ref[...]).astype(cq_ref.dtype)
    ckv_ref[...] = _rms(acc[:, Q_LORA:Q_LORA + KV_LORA], kvg_ref[...])
    p = acc[:, Q_LORA + KV_LORA:]
    kpe_ref[...] = p * cos_ref[...] + pltpu.roll(p, LANES // 2, 1) * sin_ref[...]


def mla_down(xn, w_cat, q_gain, kv_gain, cos_tab, sin_tab, tm=512):
    m, d = xn.shape
    n = w_cat.shape[1]
    period = cos_tab.shape[0]
    tm = _tile(period, _tile(m, tm))
    pblocks = period // tm
    tab_spec = pl.BlockSpec((tm, LANES), lambda i: (i % pblocks, 0))
    return pl.pallas_call(
        _mla_down_kernel,
        grid=(m // tm,),
        in_specs=[pl.BlockSpec((tm, d), lambda i: (i, 0)), pl.BlockSpec((d, n), lambda i: (0, 0)),
                  pl.BlockSpec((1, Q_LORA), lambda i: (0, 0)), pl.BlockSpec((1, KV_LORA), lambda i: (0, 0)),
                  tab_spec, tab_spec],
        out_specs=[pl.BlockSpec((tm, Q_LORA), lambda i: (i, 0)), pl.BlockSpec((tm, KV_LORA), lambda i: (i, 0)),
                   pl.BlockSpec((tm, LANES), lambda i: (i, 0))],
        out_shape=[jax.ShapeDtypeStruct((m, Q_LORA), BF16), jax.ShapeDtypeStruct((m, KV_LORA), F32),
                   jax.ShapeDtypeStruct((m, LANES), F32)],
        compiler_params=_params(1),
        name="mla_down",
    )(xn, w_cat, q_gain.reshape(1, Q_LORA), kv_gain.reshape(1, KV_LORA), cos_tab, sin_tab)


def _online_softmax_step(s, v, m_sc, l_sc, acc_sc):
    tk = s.shape[1]
    m_prev = m_sc[...]
    m_new = jnp.maximum(m_prev, jnp.max(s, axis=-1, keepdims=True))
    alpha = jnp.exp2(m_prev - m_new)
    p = jnp.exp2(s - _lane_tile(m_new, tk))
    l_sc[...] = alpha * l_sc[...] + jnp.sum(p, axis=-1, keepdims=True)
    pv = jnp.dot(p.astype(BF16), v, preferred_element_type=F32)
    acc_sc[...] = acc_sc[...] * _lane_tile(alpha, pv.shape[1]) + pv
    m_sc[...] = m_new


def _band_attn_kernel(sink_ref, q_ref, k_ref, v_ref, bias_ref, o_ref, m_sc, l_sc, acc_sc, *,
                      group, kv_heads, tq, tk, n_delta, has_sink):
    first_kv = pl.program_id(1) * kv_heads
    qb = pl.program_id(2)
    qs = []
    for j in range(kv_heads):
        cols = [(j * group + g) * HEAD_DIM for g in range(group)]
        q = jnp.concatenate([q_ref[:, c:c + HEAD_DIM] for c in cols], axis=0)
        qs.append((q * (ATTN_SCALE * LOG2E)).astype(BF16))
        if has_sink:
            for g in range(group):
                m_sc[j, g * tq:(g + 1) * tq, :] = jnp.full((tq, LANES), sink_ref[(first_kv + j) * group + g], F32)

    if has_sink:
        l_sc[...] = jnp.ones_like(l_sc)
    else:
        m_sc[...] = jnp.full_like(m_sc, NEG_BIG)
        l_sc[...] = jnp.zeros_like(l_sc)
    acc_sc[...] = jnp.zeros_like(acc_sc)

    def body(d, carry):
        start = pl.multiple_of((qb - d) * tk, tk)
        bias = bias_ref[d][None]
        for j in range(kv_heads):
            k = k_ref[pl.ds(start, tk), j * HEAD_DIM:(j + 1) * HEAD_DIM].astype(BF16)
            v = v_ref[pl.ds(start, tk), j * HEAD_DIM:(j + 1) * HEAD_DIM].astype(BF16)
            s = lax.dot_general(qs[j], k, (((1,), (1,)), ((), ())), preferred_element_type=F32)
            s = (s.reshape(group, tq, tk) + bias).reshape(group * tq, tk)
            _online_softmax_step(s, v, m_sc.at[j], l_sc.at[j], acc_sc.at[j])
        return carry

    lax.fori_loop(0, jnp.minimum(qb + 1, n_delta), body, 0)
    for j in range(kv_heads):
        out = acc_sc[j] / l_sc[j]
        for g in range(group):
            c = (j * group + g) * HEAD_DIM
            o_ref[:, c:c + HEAD_DIM] = out[g * tq:(g + 1) * tq, :].astype(o_ref.dtype)


def _band_bias(tq, tk, n_delta, patterns):
    i = np.arange(tq)[None, :, None]
    j = np.arange(tk)[None, None, :]
    dist = np.arange(n_delta)[:, None, None] * tk + i - j
    count = np.zeros(dist.shape, np.float64)
    for (w, d) in patterns:
        count += (dist >= 0) & (dist <= w) & (dist % d == 0)
    with np.errstate(divide="ignore"):
        return np.log2(count).astype(np.float32)


def band_attention(proj, seq, q_head, k_head, v_head, n_kv, group, patterns, sink, kv_heads):
    rows = proj.shape[0]
    batch = rows // seq
    tq = tk = _tile(seq, 256)
    nq = seq // tq
    reach = max(w for (w, _) in patterns)
    n_delta = min(nq, -(-reach // tk) + 1)
    bias = jnp.asarray(_band_bias(tq, tk, n_delta, patterns))
    has_sink = sink is not None
    if sink is None:
        sink = jnp.zeros((n_kv * group,), F32)
    gw = kv_heads * group * HEAD_DIM
    kw = kv_heads * HEAD_DIM
    assert n_kv % kv_heads == 0 and q_head % (kv_heads * group) == 0
    assert k_head % kv_heads == 0 and v_head % kv_heads == 0
    q_col, k_col, v_col = q_head // (kv_heads * group), k_head // kv_heads, v_head // kv_heads
    grid_spec = pltpu.PrefetchScalarGridSpec(
        num_scalar_prefetch=0,
        grid=(batch, n_kv // kv_heads, nq),
        in_specs=[
            pl.BlockSpec(memory_space=pltpu.SMEM),
            pl.BlockSpec((tq, gw), lambda b, h, i: (b * nq + i, q_col + h)),
            pl.BlockSpec((seq, kw), lambda b, h, i: (b, k_col + h)),
            pl.BlockSpec((seq, kw), lambda b, h, i: (b, v_col + h)),
            pl.BlockSpec((n_delta, tq, tk), lambda b, h, i: (0, 0, 0)),
        ],
        out_specs=pl.BlockSpec((tq, gw), lambda b, h, i: (b * nq + i, h)),
        scratch_shapes=[pltpu.VMEM((kv_heads, group * tq, LANES), F32), pltpu.VMEM((kv_heads, group * tq, LANES), F32),
                        pltpu.VMEM((kv_heads, group * tq, HEAD_DIM), F32)],
    )
    return pl.pallas_call(
        functools.partial(_band_attn_kernel, group=group, kv_heads=kv_heads, tq=tq, tk=tk, n_delta=n_delta,
                          has_sink=has_sink),
        grid_spec=grid_spec,
        out_shape=jax.ShapeDtypeStruct((rows, n_kv * group * HEAD_DIM), BF16),
        compiler_params=_params(3),
        name="band_attention",
    )(sink.astype(F32) * LOG2E, proj, proj, proj, bias)


def _mla_prompt_kernel(q_ref, kv_ref, kpe_ref, o_ref, m_sc, l_sc, acc_sc, *, tq, tk, heads):
    qb = pl.program_id(2)
    m_sc[...] = jnp.full_like(m_sc, NEG_BIG)
    l_sc[...] = jnp.zeros_like(l_sc)
    acc_sc[...] = jnp.zeros_like(acc_sc)

    def step(kb, masked):
        start = pl.multiple_of(kb * tk, tk)
        kpe = kpe_ref[pl.ds(start, tk), :]
        for j in range(heads):
            col = 2 * j * LANES
            k = jnp.concatenate([kv_ref[pl.ds(start, tk), col:col + LANES], kpe], axis=1)
            s = lax.dot_general(q_ref[:, col:col + 2 * LANES], k, (((1,), (1,)), ((), ())),
                                preferred_element_type=F32)
            if masked:
                causal = (lax.broadcasted_iota(jnp.int32, (tq, tk), 0)
                          >= lax.broadcasted_iota(jnp.int32, (tq, tk), 1))
                s = jnp.where(causal, s, -jnp.inf)
            v = kv_ref[pl.ds(start, tk), col + LANES:col + 2 * LANES]
            _online_softmax_step(s, v, m_sc.at[j], l_sc.at[j], acc_sc.at[j])

    def body(kb, carry):
        step(kb, False)
        return carry

    lax.fori_loop(0, qb, body, 0)
    step(qb, True)
    for j in range(heads):
        o_ref[:, j * V_DIM:(j + 1) * V_DIM] = (acc_sc[j] / l_sc[j]).astype(o_ref.dtype)


def mla_prompt_attention(q_full, kv, kpe, seq, heads=2):
    rows = q_full.shape[0]
    batch = rows // seq
    tq = tk = _tile(seq, 512)
    nq = seq // tq
    return pl.pallas_call(
        functools.partial(_mla_prompt_kernel, tq=tq, tk=tk, heads=heads),
        grid=(batch, H_C // heads, nq),
        in_specs=[
            pl.BlockSpec((tq, heads * 2 * LANES), lambda b, h, i: (b * nq + i, h)),
            pl.BlockSpec((seq, heads * 2 * LANES), lambda b, h, i: (b, h)),
            pl.BlockSpec((seq, LANES), lambda b, h, i: (b, 0)),
        ],
        out_specs=pl.BlockSpec((tq, heads * V_DIM), lambda b, h, i: (b * nq + i, h)),
        out_shape=jax.ShapeDtypeStruct((rows, H_C * V_DIM), BF16),
        scratch_shapes=[pltpu.VMEM((heads, tq, LANES), F32), pltpu.VMEM((heads, tq, LANES), F32),
                        pltpu.VMEM((heads, tq, V_DIM), F32)],
        compiler_params=_params(3),
        name="mla_prompt_attention",
    )(q_full, kv, kpe)


def _sample_win_kernel(q_ref, kc_ref, vc_ref, kn_ref, vn_ref, bc_ref, bn_ref, m0_ref, l0_ref, o_ref):
    q = q_ref[0]
    s_c = jnp.dot(kc_ref[0, 0].astype(BF16), q, preferred_element_type=F32) + bc_ref[...]
    s_n = jnp.dot(kn_ref[0].astype(BF16), q, preferred_element_type=F32) + bn_ref[...]
    m0 = m0_ref[...]
    m = jnp.maximum(jnp.maximum(jnp.max(s_c, axis=0, keepdims=True), jnp.max(s_n, axis=0, keepdims=True)), m0)
    p_c = jnp.exp2(s_c - m)
    p_n = jnp.exp2(s_n - m)
    den = (jnp.sum(p_c, axis=0, keepdims=True) + jnp.sum(p_n, axis=0, keepdims=True)
           + l0_ref[...] * jnp.exp2(m0 - m))
    contract_keys = (((0,), (0,)), ((), ()))
    o = (lax.dot_general(p_c.astype(BF16), vc_ref[0, 0].astype(BF16), contract_keys, preferred_element_type=F32)
         + lax.dot_general(p_n.astype(BF16), vn_ref[0].astype(BF16), contract_keys, preferred_element_type=F32))
    o_ref[0] = o / jnp.broadcast_to(den, (LANES, LANES)).T


def _sample_bias(n_cache, n_new_pad, steps, group, n_kv, patterns):
    cols = np.arange(LANES)
    col_kv = cols // (steps * group)
    t = (cols % (steps * group)) // group
    live = cols < n_kv * steps * group

    def table(pos, n_valid):
        row_pos = np.repeat(pos, n_kv)
        row_kv = np.tile(np.arange(n_kv), len(pos))
        dist = (n_cache + t)[None, :] - row_pos[:, None]
        count = np.zeros(dist.shape, np.float64)
        for (w, d) in patterns:
            count += (dist >= 0) & (dist <= w) & (dist % d == 0)
        count = np.where(row_kv[:, None] == col_kv[None, :], count, 0.0)
        with np.errstate(divide="ignore"):
            b = np.where(live[None, :], np.log2(count), 0.0)
        b[n_valid * n_kv:, :] = -np.inf
        return b.astype(np.float32)

    return (table(np.arange(n_cache), n_cache),
            table(n_cache + np.arange(n_new_pad // n_kv), steps))


def sample_window_attention(q, k_new, v_new, k_cache, v_cache, layer, n_kv, group, patterns, sink):
    db, steps, hq, _ = q.shape
    n_layers, _, n_cache = k_cache.shape[:3]
    n_cols = n_kv * steps * group
    new_pad = 16
    qcat = (q * (ATTN_SCALE * LOG2E)).reshape(db, steps, n_kv, group, HEAD_DIM).transpose(0, 4, 2, 1, 3)
    qcat = jnp.pad(qcat.reshape(db, HEAD_DIM, n_cols), ((0, 0), (0, 0), (0, LANES - n_cols))).astype(BF16)
    rows = lambda x: x.reshape(n_layers, db, n_cache * n_kv, HEAD_DIM)
    pad_new = lambda x: jnp.pad(x.reshape(db, steps * n_kv, HEAD_DIM), ((0, 0), (0, new_pad - steps * n_kv), (0, 0)))
    bias_c, bias_n = _sample_bias(n_cache, new_pad, steps, group, n_kv, patterns)
    if sink is None:
        m0 = jnp.full((1, LANES), NEG_BIG, F32)
        l0 = jnp.zeros((1, LANES), F32)
    else:
        per_col = jnp.broadcast_to(sink.reshape(n_kv, 1, group), (n_kv, steps, group)).reshape(n_cols)
        m0 = jnp.pad(per_col.astype(F32) * LOG2E, (0, LANES - n_cols)).reshape(1, LANES)
        l0 = jnp.ones((1, LANES), F32)
    cache_blk = pl.BlockSpec((1, 1, n_cache * n_kv, HEAD_DIM), lambda b: (layer, b, 0, 0))
    new_blk = pl.BlockSpec((1, new_pad, HEAD_DIM), lambda b: (b, 0, 0))
    const = lambda shape: pl.BlockSpec(shape, lambda b: (0,) * len(shape))
    out = pl.pallas_call(
        _sample_win_kernel,
        grid=(db,),
        in_specs=[pl.BlockSpec((1, HEAD_DIM, LANES), lambda b: (b, 0, 0)), cache_blk, cache_blk, new_blk, new_blk,
                  const((n_cache * n_kv, LANES)), const((new_pad, LANES)), const((1, LANES)), const((1, LANES))],
        out_specs=pl.BlockSpec((1, LANES, HEAD_DIM), lambda b: (b, 0, 0)),
        out_shape=jax.ShapeDtypeStruct((db, LANES, HEAD_DIM), F32),
        compiler_params=_params(1),
        name="sample_window_attention",
    )(qcat, rows(k_cache), rows(v_cache), pad_new(k_new), pad_new(v_new),
      jnp.asarray(bias_c), jnp.asarray(bias_n), m0, l0)
    out = out[:, :n_cols].reshape(db, n_kv, steps, group, HEAD_DIM).transpose(0, 2, 1, 3, 4)
    return out.reshape(db * steps, hq * HEAD_DIM)


def _head_matmul_kernel(x_ref, w_ref, o_ref, *, contract_w_cols):
    dims = (((1,), (1 if contract_w_cols else 0,)), ((), ()))
    o_ref[...] = lax.dot_general(x_ref[...], w_ref[0], dims, preferred_element_type=F32).astype(o_ref.dtype)


def head_matmul(x, w, layer, x_width, x_stride, w_off, out_width, contract_w_cols):
    m = x.shape[0]
    return pl.pallas_call(
        functools.partial(_head_matmul_kernel, contract_w_cols=contract_w_cols),
        grid=(H_C,),
        in_specs=[pl.BlockSpec((m, x_width), lambda h: (0, h * x_stride)),
                  pl.BlockSpec((1, KV_LORA, LANES), lambda h: (layer, 0, 2 * h + w_off))],
        out_specs=pl.BlockSpec((m, out_width), lambda h: (0, h)),
        out_shape=jax.ShapeDtypeStruct((m, H_C * out_width), BF16),
        compiler_params=_params(1),
        name="head_matmul",
    )(x, w)


def _mla_sample_kernel(*refs, pages_per_step, page, steps):
    pt_ref, ql_ref, qp_ref, cn_ref, kn_ref = refs[:5]
    ckv_refs = refs[5:5 + pages_per_step]
    kpe_refs = refs[5 + pages_per_step:5 + 2 * pages_per_step]
    o_ref, m_sc, l_sc, acc_sc, ckv_sc, kpe_sc = refs[5 + 2 * pages_per_step:]
    del pt_ref
    c = pl.program_id(1)
    contract_rows = (((0,), (0,)), ((), ()))
    n_q = acc_sc.shape[0]

    @pl.when(c == 0)
    def _():
        m_sc[...] = jnp.full_like(m_sc, NEG_BIG)
        l_sc[...] = jnp.zeros_like(l_sc)
        acc_sc[...] = jnp.zeros_like(acc_sc)

    for k in range(pages_per_step):
        ckv_sc[k * page:(k + 1) * page, :] = ckv_refs[k][0, 0].astype(BF16)
        kpe_sc[:, k * page:(k + 1) * page] = kpe_refs[k][0, 0].astype(BF16)
    ql = ql_ref[0]
    qp = qp_ref[0]

    def per_query_rows(stat):
        return _lane_tile(jnp.broadcast_to(stat, (n_q, n_q)).T, KV_LORA)

    def update(s, values):
        m_prev = m_sc[...]
        m_new = jnp.maximum(m_prev, jnp.max(s, axis=0, keepdims=True))
        alpha = jnp.exp2(m_prev - m_new)
        p = jnp.exp2(s - m_new)
        l_sc[...] = alpha * l_sc[...] + jnp.sum(p, axis=0, keepdims=True)
        pv = lax.dot_general(p.astype(BF16), values, contract_rows, preferred_element_type=F32)
        acc_sc[...] = acc_sc[...] * per_query_rows(alpha) + pv
        m_sc[...] = m_new

    ckv = ckv_sc[...]
    s = (jnp.dot(ckv, ql, preferred_element_type=F32)
         + lax.dot_general(kpe_sc[...], qp, contract_rows, preferred_element_type=F32))
    update(s, ckv)

    @pl.when(c == pl.num_programs(1) - 1)
    def _():
        cn = cn_ref[0].astype(BF16)
        s_n = (jnp.dot(cn, ql, preferred_element_type=F32)
               + jnp.dot(kn_ref[0].astype(BF16), qp, preferred_element_type=F32))
        t_key = lax.broadcasted_iota(jnp.int32, s_n.shape, 0)
        t_query = lax.broadcasted_iota(jnp.int32, s_n.shape, 1) // (n_q // steps)
        update(jnp.where((t_key <= t_query) & (t_key < steps), s_n, -jnp.inf), cn)
        o_ref[0] = (acc_sc[...] / per_query_rows(l_sc[...])).astype(o_ref.dtype)


def mla_sample_attention(q_lat, q_pe, ckv_new, kpe_new, pool_ckv, pool_kpe_t, layer, page_table, steps):
    db, _, rows = q_lat.shape
    n_pages = page_table.shape[1]
    page = pool_ckv.shape[2]
    pps = min(32, n_pages)
    assert n_pages % pps == 0 and rows == LANES
    new_pad = 16
    pad_new = lambda x: jnp.pad(x, ((0, 0), (0, new_pad - steps), (0, 0)))

    def page_spec(shape, slot):
        return pl.BlockSpec((1, 1) + shape, lambda b, c, pt: (layer, pt[b, c * pps + slot], 0, 0))

    per_seq = lambda n, w: pl.BlockSpec((1, n, w), lambda b, c, pt: (b, 0, 0))
    grid_spec = pltpu.PrefetchScalarGridSpec(
        num_scalar_prefetch=1,
        grid=(db, n_pages // pps),
        in_specs=([per_seq(KV_LORA, rows), per_seq(QK_ROPE, rows), per_seq(new_pad, KV_LORA), per_seq(new_pad, QK_ROPE)]
                  + [page_spec((page, KV_LORA), k) for k in range(pps)]
                  + [page_spec((QK_ROPE, page), k) for k in range(pps)]),
        out_specs=per_seq(rows, KV_LORA),
        scratch_shapes=[pltpu.VMEM((1, rows), F32), pltpu.VMEM((1, rows), F32),
                        pltpu.VMEM((rows, KV_LORA), F32), pltpu.VMEM((pps * page, KV_LORA), BF16),
                        pltpu.VMEM((QK_ROPE, pps * page), BF16)],
    )
    return pl.pallas_call(
        functools.partial(_mla_sample_kernel, pages_per_step=pps, page=page, steps=steps),
        grid_spec=grid_spec,
        out_shape=jax.ShapeDtypeStruct((db, rows, KV_LORA), BF16),
        compiler_params=_params(2),
        name="mla_sample_attention",
    )(page_table, q_lat, q_pe, pad_new(ckv_new), pad_new(kpe_new), *([pool_ckv] * pps), *([pool_kpe_t] * pps))


def _rope_angles(pos, dim):
    inv = ROPE_THETA ** (-jnp.arange(0, dim, 2, dtype=F32) / dim)
    ang = pos.astype(F32)[:, None] * inv[None, :]
    return jnp.cos(ang), jnp.sin(ang)


def _head_rope_tables(pos):
    c, s = _rope_angles(pos, HEAD_DIM)
    cos = jnp.stack([jnp.concatenate([c, c], 1), jnp.ones((pos.shape[0], LANES), F32)])
    sin = jnp.stack([jnp.concatenate([-s, s], 1), jnp.zeros((pos.shape[0], LANES), F32)])
    return cos, sin


def _mla_rope_tables(pos, scale=1.0):
    c, s = _rope_angles(pos, QK_ROPE)
    z = jnp.zeros_like(c)
    cos = jnp.stack([jnp.concatenate([c, z, c, z], 1), jnp.ones((pos.shape[0], LANES), F32)])
    sin = jnp.stack([jnp.concatenate([-s, z, s, z], 1), jnp.zeros((pos.shape[0], LANES), F32)])
    return cos * scale, sin * scale


def _spread_rope_cols(w):
    half = QK_ROPE // 2
    z = jnp.zeros(w.shape[:-1] + (half,), w.dtype)
    return jnp.concatenate([w[..., :half], z, w[..., half:], z], axis=-1)


def _gather_rope_cols(x):
    half = QK_ROPE // 2
    return jnp.concatenate([x[..., :half], x[..., 2 * half:3 * half]], axis=-1)


EVEN_HEAD_TYPES = np.array([0] * (HQ_A + HKV_A) + [1] * HKV_A + [0] * (HQ_B + HKV_B) + [1] * HKV_B, np.int32)
MLA_Q_HEAD_TYPES = np.array([1, 0] * H_C, np.int32)
COL_QA, COL_KA, COL_VA = 0, HQ_A, HQ_A + HKV_A
COL_QB, COL_KB, COL_VB = HQ_A + 2 * HKV_A, HQ_A + 2 * HKV_A + HQ_B, HQ_A + 2 * HKV_A + HQ_B + HKV_B


def _cols(x, first_head, n_heads):
    return x[:, first_head * HEAD_DIM:(first_head + n_heads) * HEAD_DIM]


def kernel(x_prompt, x_sample, cache_a_k, cache_a_v, cache_b_k, cache_b_v, cache_mla_ckv, cache_mla_kpe,
           page_table, norm_mix_pre, norm_mix_post, norm_ffn_pre, norm_ffn_post, w_in_e, sink_b, w_out_e,
           w_dq, q_norm, w_uq, w_dkv, kv_norm, w_ukv, w_o_mla, w_up, w_down):
    batch, seq, d_model = x_prompt.shape
    db, steps, _ = x_sample.shape
    depth = norm_mix_pre.shape[0]
    past_len = page_table.shape[1] * cache_mla_ckv.shape[2]
    keep_a = min(max(w for (w, _) in A_PATTERNS), seq)
    keep_b = min(WIN_B, seq)

    pos_p = jnp.arange(seq)
    pos_s = jnp.tile(past_len + jnp.arange(steps), db)
    rope_p, rope_s = _head_rope_tables(pos_p), _head_rope_tables(pos_s)
    mrope_p, mrope_s = _mla_rope_tables(pos_p), _mla_rope_tables(pos_s)
    mq_p, mq_s = _mla_rope_tables(pos_p, MLA_SCALE * LOG2E), _mla_rope_tables(pos_s, MLA_SCALE * LOG2E)
    even_types = jnp.asarray(EVEN_HEAD_TYPES)
    mla_q_types = jnp.asarray(MLA_Q_HEAD_TYPES)
    w_in_b, w_out_b = w_in_e.astype(BF16), w_out_e.astype(BF16)
    w_kv_b, w_o_b = w_ukv.astype(BF16), w_o_mla.astype(BF16)
    pool_kpe_t = jnp.swapaxes(cache_mla_kpe, 2, 3)

    hp = x_prompt.reshape(batch * seq, d_model)
    hs = x_sample.reshape(db * steps, d_model)
    xp = norm_cast(hp, norm_mix_pre[0])
    xs = norm_cast(hs, norm_mix_pre[0])
    outs = {name: [] for name in ("p_ak", "p_av", "p_bk", "p_bv", "p_ckv", "p_kpe",
                                  "s_ak", "s_av", "s_bk", "s_bv", "s_ckv", "s_kpe")}

    for layer in range(depth):
        li = layer // 2
        if layer % 2 == 0:
            proj = matmul_rope(xp, w_in_b, li, even_types, *rope_p, F32)
            oa = band_attention(proj, seq, COL_QA, COL_KA, COL_VA, HKV_A, HQ_A // HKV_A, A_PATTERNS, None, 2)
            ob = band_attention(proj, seq, COL_QB, COL_KB, COL_VB, HKV_B, HQ_B // HKV_B, ((WIN_B, 1),),
                                sink_b[li], 1)
            mix_p = matmul([oa, ob], w_out_b, li, F32)
            per_seq = lambda x, n: x.reshape(batch, seq, n, HEAD_DIM)
            outs["p_ak"].append(per_seq(_cols(proj, COL_KA, HKV_A), HKV_A)[:, seq - keep_a:])
            outs["p_av"].append(per_seq(_cols(proj, COL_VA, HKV_A), HKV_A)[:, seq - keep_a:])
            outs["p_bk"].append(per_seq(_cols(proj, COL_KB, HKV_B), HKV_B)[:, seq - keep_b:])
            outs["p_bv"].append(per_seq(_cols(proj, COL_VB, HKV_B), HKV_B)[:, seq - keep_b:])
            proj = matmul_rope(xs, w_in_b, li, even_types, *rope_s, F32)
            per_tok = lambda x, n: x.reshape(db, steps, n, HEAD_DIM)
            ka, va = per_tok(_cols(proj, COL_KA, HKV_A), HKV_A), per_tok(_cols(proj, COL_VA, HKV_A), HKV_A)
            kb, vb = per_tok(_cols(proj, COL_KB, HKV_B), HKV_B), per_tok(_cols(proj, COL_VB, HKV_B), HKV_B)
            oa = sample_window_attention(per_tok(_cols(proj, COL_QA, HQ_A), HQ_A), ka, va, cache_a_k, cache_a_v,
                                         li, HKV_A, HQ_A // HKV_A, A_PATTERNS, None)
            ob = sample_window_attention(per_tok(_cols(proj, COL_QB, HQ_B), HQ_B), kb, vb, cache_b_k, cache_b_v,
                                         li, HKV_B, HQ_B // HKV_B, ((WIN_B, 1),), sink_b[li])
            mix_s = matmul([oa, ob], w_out_b, li, F32)
            outs["s_ak"].append(ka); outs["s_av"].append(va); outs["s_bk"].append(kb); outs["s_bv"].append(vb)
        else:
            w_cat = jnp.concatenate([w_dq[li], w_dkv[li][:, :KV_LORA], _spread_rope_cols(w_dkv[li][:, KV_LORA:])],
                                    axis=1).astype(BF16)
            w_uq_l = w_uq[li].reshape(Q_LORA, H_C, QK_NOPE + QK_ROPE)
            w_q = jnp.concatenate([w_uq_l[..., :QK_NOPE], _spread_rope_cols(w_uq_l[..., QK_NOPE:])], axis=-1)
            w_q = w_q.reshape(1, Q_LORA, H_C * 2 * LANES).astype(BF16)
            cq, ckv, kpe = mla_down(xp, w_cat, q_norm[li], kv_norm[li], mrope_p[0][0], mrope_p[1][0])
            q_full = matmul_rope(cq, w_q, 0, mla_q_types, *mq_p, BF16, tn=1024)
            kv = matmul([ckv], w_kv_b, li, BF16, tn=1024)
            o = mla_prompt_attention(q_full, kv, kpe.astype(BF16), seq)
            mix_p = matmul([o], w_o_b, li, F32)
            outs["p_ckv"].append(ckv.reshape(batch, seq, KV_LORA))
            outs["p_kpe"].append(_gather_rope_cols(kpe).reshape(batch, seq, QK_ROPE))
            cq, ckv, kpe = mla_down(xs, w_cat, q_norm[li], kv_norm[li], mrope_s[0][0], mrope_s[1][0])
            q_full = matmul_rope(cq, w_q, 0, mla_q_types, *mq_s, BF16, tn=1024)
            q_lat = head_matmul(q_full, w_kv_b, li, LANES, 2, 0, KV_LORA, True)
            q_pe = _gather_rope_cols(q_full.reshape(db * steps, H_C, 2, LANES)[:, :, 1, :])
            ckv_s = ckv.reshape(db, steps, KV_LORA)
            kpe_s = _gather_rope_cols(kpe).reshape(db, steps, QK_ROPE)
            queries_as_cols = lambda x, w: x.reshape(db, steps * H_C, w).transpose(0, 2, 1)
            o_lat = mla_sample_attention(queries_as_cols(q_lat, KV_LORA), queries_as_cols(q_pe, QK_ROPE), ckv_s, kpe_s,
                                         cache_mla_ckv, pool_kpe_t, li, page_table, steps)
            o = head_matmul(o_lat.reshape(db * steps, H_C * KV_LORA), w_kv_b, li, KV_LORA, 1, 1, V_DIM, False)
            mix_s = matmul([o], w_o_b, li, F32)
            outs["s_ckv"].append(ckv_s); outs["s_kpe"].append(kpe_s)

        hp, xp = resid_norm(hp, mix_p, norm_mix_post[layer], norm_ffn_pre[layer])
        hs, xs = resid_norm(hs, mix_s, norm_mix_post[layer], norm_ffn_pre[layer])
        g_next = norm_mix_pre[layer + 1] if layer + 1 < depth else None
        mlp_s, w_u, w_d = mlp_cast(xs, w_up, w_down, layer)
        hs, xs = resid_norm(hs, mlp_s, norm_ffn_post[layer], g_next)
        hp, xp = resid_norm(hp, mlp(xp, w_u, w_d), norm_ffn_post[layer], g_next)

    stack = lambda name: jnp.stack(outs[name])
    return (hp.reshape(batch, seq, d_model), hs.reshape(db, steps, d_model),
            stack("p_ak"), stack("p_av"), stack("p_bk"), stack("p_bv"), stack("p_ckv"), stack("p_kpe"),
            stack("s_ak"), stack("s_av"), stack("s_bk"), stack("s_bv"), stack("s_ckv"), stack("s_kpe"))
```

```python
import functools

import numpy as np
import jax
import jax.numpy as jnp
from jax import lax
from jax.experimental import pallas as pl
from jax.experimental.pallas import tpu as pltpu

F32 = jnp.float32
BF16 = jnp.bfloat16

HEAD_DIM = 128
LANES = 128
HQ_A, HKV_A = 16, 4
HQ_B, HKV_B = 16, 2
A_PATTERNS = ((128, 1), (512, 4), (2048, 16))
WIN_B = 128
H_C = 32
Q_LORA, KV_LORA = 1024, 512
QK_NOPE, QK_ROPE, V_DIM = 128, 64, 128
MLA_SCALE = (QK_NOPE + QK_ROPE) ** -0.5
ATTN_SCALE = HEAD_DIM ** -0.5
LOG2E = 1.4426950408889634
ROPE_THETA = 10000.0
EPS = 1e-6
NEG_BIG = -1e30
VMEM_LIMIT = 56 * 1024 * 1024


def _params(n_axes):
    return pltpu.CompilerParams(dimension_semantics=("arbitrary",) * n_axes, vmem_limit_bytes=VMEM_LIMIT)


def _tile(n, t):
    t = min(n, t)
    assert n % t == 0, (n, t)
    return t


def _rms(x, g):
    return x * lax.rsqrt(jnp.mean(x * x, axis=-1, keepdims=True) + EPS) * g


def _lane_tile(x, width):
    reps = width // LANES
    return x if reps == 1 else jnp.concatenate([x] * reps, axis=1)


def _norm_cast_kernel(x_ref, g_ref, o_ref):
    o_ref[...] = _rms(x_ref[...], g_ref[...]).astype(o_ref.dtype)


def norm_cast(x, g):
    r, d = x.shape
    tr = _tile(r, 256)
    return pl.pallas_call(
        _norm_cast_kernel,
        grid=(r // tr,),
        in_specs=[pl.BlockSpec((tr, d), lambda i: (i, 0)), pl.BlockSpec((1, d), lambda i: (0, 0))],
        out_specs=pl.BlockSpec((tr, d), lambda i: (i, 0)),
        out_shape=jax.ShapeDtypeStruct((r, d), BF16),
        compiler_params=_params(1),
        name="norm_cast",
    )(x, g.reshape(1, d))


def _resid_norm_kernel(h_ref, m_ref, gp_ref, gn_ref, ho_ref, xn_ref):
    h = h_ref[...] + _rms(m_ref[...], gp_ref[...])
    ho_ref[...] = h
    xn_ref[...] = _rms(h, gn_ref[...]).astype(xn_ref.dtype)


def _resid_kernel(h_ref, m_ref, gp_ref, ho_ref):
    ho_ref[...] = h_ref[...] + _rms(m_ref[...], gp_ref[...])


def resid_norm(h, m, g_post, g_next):
    r, d = h.shape
    tr = _tile(r, 256)
    row = pl.BlockSpec((tr, d), lambda i: (i, 0))
    vec = pl.BlockSpec((1, d), lambda i: (0, 0))
    if g_next is None:
        return pl.pallas_call(
            _resid_kernel, grid=(r // tr,), in_specs=[row, row, vec], out_specs=row,
            out_shape=jax.ShapeDtypeStruct((r, d), F32), compiler_params=_params(1), name="resid",
        )(h, m, g_post.reshape(1, d)), None
    return pl.pallas_call(
        _resid_norm_kernel, grid=(r // tr,), in_specs=[row, row, vec, vec], out_specs=[row, row],
        out_shape=[jax.ShapeDtypeStruct((r, d), F32), jax.ShapeDtypeStruct((r, d), BF16)],
        compiler_params=_params(1), name="resid_norm",
    )(h, m, g_post.reshape(1, d), g_next.reshape(1, d))


def _bf16_weights(w_ref, wb_ref):
    if wb_ref is None:
        return w_ref
    wb_ref[...] = w_ref[...].astype(BF16)
    return wb_ref


def _matmul_kernel(*refs, emit_bf16):
    if emit_bf16:
        *x_refs, w_ref, o_ref, wb_ref = refs
    else:
        *x_refs, w_ref, o_ref = refs
        wb_ref = None
    w_ref = _bf16_weights(w_ref, wb_ref)
    acc, k0 = None, 0
    for x_ref in x_refs:
        k1 = k0 + x_ref.shape[1]
        part = jnp.dot(x_ref[...].astype(BF16), w_ref[0, k0:k1, :], preferred_element_type=F32)
        acc = part if acc is None else acc + part
        k0 = k1
    o_ref[...] = acc.astype(o_ref.dtype)


def _weight_out(emit_bf16, m, tm, k, n, tn, index_map):
    if not emit_bf16:
        return [], []
    assert m == tm
    return [pl.BlockSpec((1, k, tn), index_map)], [jax.ShapeDtypeStruct((1, k, n), BF16)]


def matmul(xs, w, layer, out_dtype, tm=1024, tn=512, emit_bf16=False):
    m = xs[0].shape[0]
    k, n = w.shape[1:]
    assert sum(x.shape[1] for x in xs) == k
    tm, tn = _tile(m, tm), _tile(n, tn)
    wb_spec, wb_shape = _weight_out(emit_bf16, m, tm, k, n, tn, lambda i, j: (0, 0, j))
    out = pl.pallas_call(
        functools.partial(_matmul_kernel, emit_bf16=emit_bf16),
        grid=(m // tm, n // tn),
        in_specs=([pl.BlockSpec((tm, x.shape[1]), lambda i, j: (i, 0)) for x in xs]
                  + [pl.BlockSpec((1, k, tn), lambda i, j: (layer, 0, j))]),
        out_specs=[pl.BlockSpec((tm, tn), lambda i, j: (i, j))] + wb_spec,
        out_shape=[jax.ShapeDtypeStruct((m, n), out_dtype)] + wb_shape,
        compiler_params=_params(2),
        name="matmul",
    )(*xs, w)
    return out if emit_bf16 else out[0]


def _matmul_rope_kernel(types_ref, x_ref, w_ref, cos_ref, sin_ref, o_ref, wb_ref=None, *, heads_per_block):
    j = pl.program_id(1)
    w_ref = _bf16_weights(w_ref, wb_ref)
    acc = jnp.dot(x_ref[...].astype(BF16), w_ref[0], preferred_element_type=F32)
    for hh in range(heads_per_block):
        t = types_ref[j * heads_per_block + hh]
        a = acc[:, hh * LANES:(hh + 1) * LANES]
        roped = a * cos_ref[t] + pltpu.roll(a, LANES // 2, 1) * sin_ref[t]
        o_ref[:, hh * LANES:(hh + 1) * LANES] = roped.astype(o_ref.dtype)


def matmul_rope(x, w, layer, head_types, cos_tab, sin_tab, out_dtype, tm=1024, tn=512, emit_bf16=False):
    m, k = x.shape
    n = w.shape[2]
    n_types, period, _ = cos_tab.shape
    tm, tn = _tile(m, tm), _tile(n, tn)
    tm = _tile(period, tm)
    assert m % tm == 0
    pblocks = period // tm
    tab_spec = pl.BlockSpec((n_types, tm, LANES), lambda i, j, t: (0, i % pblocks, 0))
    wb_spec, wb_shape = _weight_out(emit_bf16, m, tm, k, n, tn, lambda i, j, t: (0, 0, j))
    grid_spec = pltpu.PrefetchScalarGridSpec(
        num_scalar_prefetch=1,
        grid=(m // tm, n // tn),
        in_specs=[pl.BlockSpec((tm, k), lambda i, j, t: (i, 0)),
                  pl.BlockSpec((1, k, tn), lambda i, j, t: (layer, 0, j)), tab_spec, tab_spec],
        out_specs=[pl.BlockSpec((tm, tn), lambda i, j, t: (i, j))] + wb_spec,
    )
    out = pl.pallas_call(
        functools.partial(_matmul_rope_kernel, heads_per_block=tn // LANES),
        grid_spec=grid_spec,
        out_shape=[jax.ShapeDtypeStruct((m, n), out_dtype)] + wb_shape,
        compiler_params=_params(2),
        name="matmul_rope",
    )(head_types, x, w, cos_tab, sin_tab)
    return out if emit_bf16 else out[0]


MLP_OUT_CHUNK = 1024


def _mlp_step(x_ref, wu_ref, wd_ref, o_ref, first):
    @pl.when(first)
    def _():
        o_ref[...] = jnp.zeros_like(o_ref)

    h = jnp.dot(x_ref[...], wu_ref[...], preferred_element_type=F32)
    h = jnp.square(jnp.maximum(h, 0.0)).astype(BF16)
    d = o_ref.shape[1]
    tn = min(d, MLP_OUT_CHUNK)
    for c in range(d // tn):
        o_ref[:, c * tn:(c + 1) * tn] += jnp.dot(h, wd_ref[:, c * tn:(c + 1) * tn], preferred_element_type=F32)


def _mlp_kernel(x_ref, wu_ref, wd_ref, o_ref):
    _mlp_step(x_ref, wu_ref, wd_ref, o_ref, pl.program_id(1) == 0)


def _mlp_cast_kernel(x_ref, wu_ref, wd_ref, o_ref, wub_ref, wdb_ref):
    wub_ref[...] = wu_ref[0].astype(BF16)
    wdb_ref[...] = wd_ref[0].astype(BF16)
    _mlp_step(x_ref, wub_ref, wdb_ref, o_ref, pl.program_id(0) == 0)


def mlp_cast(xn, w_up, w_down, layer, tf=256):
    m, d = xn.shape
    ff = w_up.shape[2]
    tf = _tile(ff, tf)
    return pl.pallas_call(
        _mlp_cast_kernel,
        grid=(ff // tf,),
        in_specs=[pl.BlockSpec((m, d), lambda f: (0, 0)), pl.BlockSpec((1, d, tf), lambda f: (layer, 0, f)),
                  pl.BlockSpec((1, tf, d), lambda f: (layer, f, 0))],
        out_specs=[pl.BlockSpec((m, d), lambda f: (0, 0)), pl.BlockSpec((d, tf), lambda f: (0, f)),
                   pl.BlockSpec((tf, d), lambda f: (f, 0))],
        out_shape=[jax.ShapeDtypeStruct((m, d), F32), jax.ShapeDtypeStruct((d, ff), BF16),
                   jax.ShapeDtypeStruct((ff, d), BF16)],
        compiler_params=_params(1),
        name="mlp_cast",
    )(xn, w_up, w_down)


def mlp(xn, w_up, w_down, tm=512, tf=1024):
    m, d = xn.shape
    ff = w_up.shape[1]
    tm, tf = _tile(m, tm), _tile(ff, tf)
    once = pl.Buffered(1)
    return pl.pallas_call(
        _mlp_kernel,
        grid=(m // tm, ff // tf),
        in_specs=[pl.BlockSpec((tm, d), lambda i, f: (i, 0), pipeline_mode=once),
                  pl.BlockSpec((d, tf), lambda i, f: (0, f)), pl.BlockSpec((tf, d), lambda i, f: (f, 0))],
        out_specs=pl.BlockSpec((tm, d), lambda i, f: (i, 0), pipeline_mode=once),
        out_shape=jax.ShapeDtypeStruct((m, d), F32),
        compiler_params=_params(2),
        name="mlp",
    )(xn, w_up, w_down)


def _mla_down_kernel(x_ref, w_ref, qg_ref, kvg_ref, cos_ref, sin_ref, cq_ref, ckv_ref, kpe_ref):
    acc = jnp.dot(x_ref[...], w_ref[...], preferred_element_type=F32)
    cq_ref[...] = _rms(acc[:, :Q_LORA], qg_ref[...]).astype(cq_ref.dtype)
    ckv_ref[...] = _rms(acc[:, Q_LORA:Q_LORA + KV_LORA], kvg_ref[...])
    p = acc[:, Q_LORA + KV_LORA:]
    kpe_ref[...] = p * cos_ref[...] + pltpu.roll(p, LANES // 2, 1) * sin_ref[...]


def mla_down(xn, w_cat, q_gain, kv_gain, cos_tab, sin_tab, tm=512):
    m, d = xn.shape
    n = w_cat.shape[1]
    period = cos_tab.shape[0]
    tm = _tile(period, _tile(m, tm))
    pblocks = period // tm
    tab_spec = pl.BlockSpec((tm, LANES), lambda i: (i % pblocks, 0))
    return pl.pallas_call(
        _mla_down_kernel,
        grid=(m // tm,),
        in_specs=[pl.BlockSpec((tm, d), lambda i: (i, 0)), pl.BlockSpec((d, n), lambda i: (0, 0)),
                  pl.BlockSpec((1, Q_LORA), lambda i: (0, 0)), pl.BlockSpec((1, KV_LORA), lambda i: (0, 0)),
                  tab_spec, tab_spec],
        out_specs=[pl.BlockSpec((tm, Q_LORA), lambda i: (i, 0)), pl.BlockSpec((tm, KV_LORA), lambda i: (i, 0)),
                   pl.BlockSpec((tm, LANES), lambda i: (i, 0))],
        out_shape=[jax.ShapeDtypeStruct((m, Q_LORA), BF16), jax.ShapeDtypeStruct((m, KV_LORA), F32),
                   jax.ShapeDtypeStruct((m, LANES), F32)],
        compiler_params=_params(1),
        name="mla_down",
    )(xn, w_cat, q_gain.reshape(1, Q_LORA), kv_gain.reshape(1, KV_LORA), cos_tab, sin_tab)


def _online_softmax_step(s, v, m_sc, l_sc, acc_sc):
    tk = s.shape[1]
    m_prev = m_sc[...]
    m_new = jnp.maximum(m_prev, jnp.max(s, axis=-1, keepdims=True))
    alpha = jnp.exp2(m_prev - m_new)
    p = jnp.exp2(s - _lane_tile(m_new, tk))
    l_sc[...] = alpha * l_sc[...] + jnp.sum(p, axis=-1, keepdims=True)
    pv = jnp.dot(p.astype(BF16), v, preferred_element_type=F32)
    acc_sc[...] = acc_sc[...] * _lane_tile(alpha, pv.shape[1]) + pv
    m_sc[...] = m_new


def _band_attn_kernel(sink_ref, q_ref, k_ref, v_ref, bias_ref, o_ref, m_sc, l_sc, acc_sc, *,
                      group, kv_heads, tq, tk, n_delta, has_sink):
    first_kv = pl.program_id(1) * kv_heads
    qb = pl.program_id(2)
    qs = []
    for j in range(kv_heads):
        cols = [(j * group + g) * HEAD_DIM for g in range(group)]
        q = jnp.concatenate([q_ref[:, c:c + HEAD_DIM] for c in cols], axis=0)
        qs.append((q * (ATTN_SCALE * LOG2E)).astype(BF16))
        if has_sink:
            for g in range(group):
                m_sc[j, g * tq:(g + 1) * tq, :] = jnp.full((tq, LANES), sink_ref[(first_kv + j) * group + g], F32)

    if has_sink:
        l_sc[...] = jnp.ones_like(l_sc)
    else:
        m_sc[...] = jnp.full_like(m_sc, NEG_BIG)
        l_sc[...] = jnp.zeros_like(l_sc)
    acc_sc[...] = jnp.zeros_like(acc_sc)

    def body(d, carry):
        start = pl.multiple_of((qb - d) * tk, tk)
        bias = bias_ref[d][None]
        for j in range(kv_heads):
            k = k_ref[pl.ds(start, tk), j * HEAD_DIM:(j + 1) * HEAD_DIM].astype(BF16)
            v = v_ref[pl.ds(start, tk), j * HEAD_DIM:(j + 1) * HEAD_DIM].astype(BF16)
            s = lax.dot_general(qs[j], k, (((1,), (1,)), ((), ())), preferred_element_type=F32)
            s = (s.reshape(group, tq, tk) + bias).reshape(group * tq, tk)
            _online_softmax_step(s, v, m_sc.at[j], l_sc.at[j], acc_sc.at[j])
        return carry

    lax.fori_loop(0, jnp.minimum(qb + 1, n_delta), body, 0)
    for j in range(kv_heads):
        out = acc_sc[j] / l_sc[j]
        for g in range(group):
            c = (j * group + g) * HEAD_DIM
            o_ref[:, c:c + HEAD_DIM] = out[g * tq:(g + 1) * tq, :].astype(o_ref.dtype)


def _band_bias(tq, tk, n_delta, patterns):
    i = np.arange(tq)[None, :, None]
    j = np.arange(tk)[None, None, :]
    dist = np.arange(n_delta)[:, None, None] * tk + i - j
    count = np.zeros(dist.shape, np.float64)
    for (w, d) in patterns:
        count += (dist >= 0) & (dist <= w) & (dist % d == 0)
    with np.errstate(divide="ignore"):
        return np.log2(count).astype(np.float32)


def band_attention(proj, seq, q_head, k_head, v_head, n_kv, group, patterns, sink, kv_heads):
    rows = proj.shape[0]
    batch = rows // seq
    tq = tk = _tile(seq, 256)
    nq = seq // tq
    reach = max(w for (w, _) in patterns)
    n_delta = min(nq, -(-reach // tk) + 1)
    bias = jnp.asarray(_band_bias(tq, tk, n_delta, patterns))
    has_sink = sink is not None
    if sink is None:
        sink = jnp.zeros((n_kv * group,), F32)
    gw = kv_heads * group * HEAD_DIM
    kw = kv_heads * HEAD_DIM
    assert n_kv % kv_heads == 0 and q_head % (kv_heads * group) == 0
    assert k_head % kv_heads == 0 and v_head % kv_heads == 0
    q_col, k_col, v_col = q_head // (kv_heads * group), k_head // kv_heads, v_head // kv_heads
    grid_spec = pltpu.PrefetchScalarGridSpec(
        num_scalar_prefetch=0,
        grid=(batch, n_kv // kv_heads, nq),
        in_specs=[
            pl.BlockSpec(memory_space=pltpu.SMEM),
            pl.BlockSpec((tq, gw), lambda b, h, i: (b * nq + i, q_col + h)),
            pl.BlockSpec((seq, kw), lambda b, h, i: (b, k_col + h)),
            pl.BlockSpec((seq, kw), lambda b, h, i: (b, v_col + h)),
            pl.BlockSpec((n_delta, tq, tk), lambda b, h, i: (0, 0, 0)),
        ],
        out_specs=pl.BlockSpec((tq, gw), lambda b, h, i: (b * nq + i, h)),
        scratch_shapes=[pltpu.VMEM((kv_heads, group * tq, LANES), F32), pltpu.VMEM((kv_heads, group * tq, LANES), F32),
                        pltpu.VMEM((kv_heads, group * tq, HEAD_DIM), F32)],
    )
    return pl.pallas_call(
        functools.partial(_band_attn_kernel, group=group, kv_heads=kv_heads, tq=tq, tk=tk, n_delta=n_delta,
                          has_sink=has_sink),
        grid_spec=grid_spec,
        out_shape=jax.ShapeDtypeStruct((rows, n_kv * group * HEAD_DIM), BF16),
        compiler_params=_params(3),
        name="band_attention",
    )(sink.astype(F32) * LOG2E, proj, proj, proj, bias)


def _mla_prompt_kernel(q_ref, kv_ref, kpe_ref, o_ref, m_sc, l_sc, acc_sc, *, tq, tk, heads):
    qb = pl.program_id(2)
    m_sc[...] = jnp.full_like(m_sc, NEG_BIG)
    l_sc[...] = jnp.zeros_like(l_sc)
    acc_sc[...] = jnp.zeros_like(acc_sc)

    def step(kb, masked):
        start = pl.multiple_of(kb * tk, tk)
        kpe = kpe_ref[pl.ds(start, tk), :]
        for j in range(heads):
            col = 2 * j * LANES
            k = jnp.concatenate([kv_ref[pl.ds(start, tk), col:col + LANES], kpe], axis=1)
            s = lax.dot_general(q_ref[:, col:col + 2 * LANES], k, (((1,), (1,)), ((), ())),
                                preferred_element_type=F32)
            if masked:
                causal = (lax.broadcasted_iota(jnp.int32, (tq, tk), 0)
                          >= lax.broadcasted_iota(jnp.int32, (tq, tk), 1))
                s = jnp.where(causal, s, -jnp.inf)
            v = kv_ref[pl.ds(start, tk), col + LANES:col + 2 * LANES]
            _online_softmax_step(s, v, m_sc.at[j], l_sc.at[j], acc_sc.at[j])

    def body(kb, carry):
        step(kb, False)
        return carry

    lax.fori_loop(0, qb, body, 0)
    step(qb, True)
    for j in range(heads):
        o_ref[:, j * V_DIM:(j + 1) * V_DIM] = (acc_sc[j] / l_sc[j]).astype(o_ref.dtype)


def mla_prompt_attention(q_full, kv, kpe, seq, heads=4):
    rows = q_full.shape[0]
    batch = rows // seq
    tq = tk = _tile(seq, 512)
    nq = seq // tq
    return pl.pallas_call(
        functools.partial(_mla_prompt_kernel, tq=tq, tk=tk, heads=heads),
        grid=(batch, H_C // heads, nq),
        in_specs=[
            pl.BlockSpec((tq, heads * 2 * LANES), lambda b, h, i: (b * nq + i, h)),
            pl.BlockSpec((seq, heads * 2 * LANES), lambda b, h, i: (b, h)),
            pl.BlockSpec((seq, LANES), lambda b, h, i: (b, 0)),
        ],
        out_specs=pl.BlockSpec((tq, heads * V_DIM), lambda b, h, i: (b * nq + i, h)),
        out_shape=jax.ShapeDtypeStruct((rows, H_C * V_DIM), BF16),
        scratch_shapes=[pltpu.VMEM((heads, tq, LANES), F32), pltpu.VMEM((heads, tq, LANES), F32),
                        pltpu.VMEM((heads, tq, V_DIM), F32)],
        compiler_params=_params(3),
        name="mla_prompt_attention",
    )(q_full, kv, kpe)


def _sample_win_kernel(q_ref, kc_ref, vc_ref, kn_ref, vn_ref, bc_ref, bn_ref, m0_ref, l0_ref, o_ref):
    q = q_ref[0]
    s_c = jnp.dot(kc_ref[0, 0].astype(BF16), q, preferred_element_type=F32) + bc_ref[...]
    s_n = jnp.dot(kn_ref[0].astype(BF16), q, preferred_element_type=F32) + bn_ref[...]
    m0 = m0_ref[...]
    m = jnp.maximum(jnp.maximum(jnp.max(s_c, axis=0, keepdims=True), jnp.max(s_n, axis=0, keepdims=True)), m0)
    p_c = jnp.exp2(s_c - m)
    p_n = jnp.exp2(s_n - m)
    den = (jnp.sum(p_c, axis=0, keepdims=True) + jnp.sum(p_n, axis=0, keepdims=True)
           + l0_ref[...] * jnp.exp2(m0 - m))
    contract_keys = (((0,), (0,)), ((), ()))
    o = (lax.dot_general(p_c.astype(BF16), vc_ref[0, 0].astype(BF16), contract_keys, preferred_element_type=F32)
         + lax.dot_general(p_n.astype(BF16), vn_ref[0].astype(BF16), contract_keys, preferred_element_type=F32))
    o_ref[0] = o / jnp.broadcast_to(den, (LANES, LANES)).T


def _sample_bias(n_cache, n_new_pad, steps, group, n_kv, patterns):
    cols = np.arange(LANES)
    col_kv = cols // (steps * group)
    t = (cols % (steps * group)) // group
    live = cols < n_kv * steps * group

    def table(pos, n_valid):
        row_pos = np.repeat(pos, n_kv)
        row_kv = np.tile(np.arange(n_kv), len(pos))
        dist = (n_cache + t)[None, :] - row_pos[:, None]
        count = np.zeros(dist.shape, np.float64)
        for (w, d) in patterns:
            count += (dist >= 0) & (dist <= w) & (dist % d == 0)
        count = np.where(row_kv[:, None] == col_kv[None, :], count, 0.0)
        with np.errstate(divide="ignore"):
            b = np.where(live[None, :], np.log2(count), 0.0)
        b[n_valid * n_kv:, :] = -np.inf
        return b.astype(np.float32)

    return (table(np.arange(n_cache), n_cache),
            table(n_cache + np.arange(n_new_pad // n_kv), steps))


def sample_window_attention(q, k_new, v_new, k_cache, v_cache, layer, n_kv, group, patterns, sink):
    db, steps, hq, _ = q.shape
    n_layers, _, n_cache = k_cache.shape[:3]
    n_cols = n_kv * steps * group
    new_pad = 16
    qcat = (q * (ATTN_SCALE * LOG2E)).reshape(db, steps, n_kv, group, HEAD_DIM).transpose(0, 4, 2, 1, 3)
    qcat = jnp.pad(qcat.reshape(db, HEAD_DIM, n_cols), ((0, 0), (0, 0), (0, LANES - n_cols))).astype(BF16)
    rows = lambda x: x.reshape(n_layers, db, n_cache * n_kv, HEAD_DIM)
    pad_new = lambda x: jnp.pad(x.reshape(db, steps * n_kv, HEAD_DIM), ((0, 0), (0, new_pad - steps * n_kv), (0, 0)))
    bias_c, bias_n = _sample_bias(n_cache, new_pad, steps, group, n_kv, patterns)
    if sink is None:
        m0 = jnp.full((1, LANES), NEG_BIG, F32)
        l0 = jnp.zeros((1, LANES), F32)
    else:
        per_col = jnp.broadcast_to(sink.reshape(n_kv, 1, group), (n_kv, steps, group)).reshape(n_cols)
        m0 = jnp.pad(per_col.astype(F32) * LOG2E, (0, LANES - n_cols)).reshape(1, LANES)
        l0 = jnp.ones((1, LANES), F32)
    cache_blk = pl.BlockSpec((1, 1, n_cache * n_kv, HEAD_DIM), lambda b: (layer, b, 0, 0))
    new_blk = pl.BlockSpec((1, new_pad, HEAD_DIM), lambda b: (b, 0, 0))
    const = lambda shape: pl.BlockSpec(shape, lambda b: (0,) * len(shape))
    out = pl.pallas_call(
        _sample_win_kernel,
        grid=(db,),
        in_specs=[pl.BlockSpec((1, HEAD_DIM, LANES), lambda b: (b, 0, 0)), cache_blk, cache_blk, new_blk, new_blk,
                  const((n_cache * n_kv, LANES)), const((new_pad, LANES)), const((1, LANES)), const((1, LANES))],
        out_specs=pl.BlockSpec((1, LANES, HEAD_DIM), lambda b: (b, 0, 0)),
        out_shape=jax.ShapeDtypeStruct((db, LANES, HEAD_DIM), F32),
        compiler_params=_params(1),
        name="sample_window_attention",
    )(qcat, rows(k_cache), rows(v_cache), pad_new(k_new), pad_new(v_new),
      jnp.asarray(bias_c), jnp.asarray(bias_n), m0, l0)
    out = out[:, :n_cols].reshape(db, n_kv, steps, group, HEAD_DIM).transpose(0, 2, 1, 3, 4)
    return out.reshape(db * steps, hq * HEAD_DIM)


def _head_matmul_kernel(x_ref, w_ref, o_ref, *, contract_w_cols):
    dims = (((1,), (1 if contract_w_cols else 0,)), ((), ()))
    o_ref[...] = lax.dot_general(x_ref[...], w_ref[0], dims, preferred_element_type=F32).astype(o_ref.dtype)


def head_matmul(x, w, layer, x_width, x_stride, w_off, out_width, contract_w_cols):
    m = x.shape[0]
    return pl.pallas_call(
        functools.partial(_head_matmul_kernel, contract_w_cols=contract_w_cols),
        grid=(H_C,),
        in_specs=[pl.BlockSpec((m, x_width), lambda h: (0, h * x_stride)),
                  pl.BlockSpec((1, KV_LORA, LANES), lambda h: (layer, 0, 2 * h + w_off))],
        out_specs=pl.BlockSpec((m, out_width), lambda h: (0, h)),
        out_shape=jax.ShapeDtypeStruct((m, H_C * out_width), BF16),
        compiler_params=_params(1),
        name="head_matmul",
    )(x, w)


def _mla_sample_kernel(*refs, pages_per_step, page, steps):
    pt_ref, ql_ref, qp_ref, cn_ref, kn_ref = refs[:5]
    ckv_refs = refs[5:5 + pages_per_step]
    kpe_refs = refs[5 + pages_per_step:5 + 2 * pages_per_step]
    o_ref, m_sc, l_sc, acc_sc, ckv_sc, kpe_sc = refs[5 + 2 * pages_per_step:]
    del pt_ref
    c = pl.program_id(1)
    contract_rows = (((0,), (0,)), ((), ()))
    n_q = acc_sc.shape[0]

    @pl.when(c == 0)
    def _():
        m_sc[...] = jnp.full_like(m_sc, NEG_BIG)
        l_sc[...] = jnp.zeros_like(l_sc)
        acc_sc[...] = jnp.zeros_like(acc_sc)

    for k in range(pages_per_step):
        ckv_sc[k * page:(k + 1) * page, :] = ckv_refs[k][0, 0].astype(BF16)
        kpe_sc[:, k * page:(k + 1) * page] = kpe_refs[k][0, 0].astype(BF16)
    ql = ql_ref[0]
    qp = qp_ref[0]

    def per_query_rows(stat):
        return _lane_tile(jnp.broadcast_to(stat, (n_q, n_q)).T, KV_LORA)

    def update(s, values):
        m_prev = m_sc[...]
        m_new = jnp.maximum(m_prev, jnp.max(s, axis=0, keepdims=True))
        alpha = jnp.exp2(m_prev - m_new)
        p = jnp.exp2(s - m_new)
        l_sc[...] = alpha * l_sc[...] + jnp.sum(p, axis=0, keepdims=True)
        pv = lax.dot_general(p.astype(BF16), values, contract_rows, preferred_element_type=F32)
        acc_sc[...] = acc_sc[...] * per_query_rows(alpha) + pv
        m_sc[...] = m_new

    ckv = ckv_sc[...]
    s = (jnp.dot(ckv, ql, preferred_element_type=F32)
         + lax.dot_general(kpe_sc[...], qp, contract_rows, preferred_element_type=F32))
    update(s, ckv)

    @pl.when(c == pl.num_programs(1) - 1)
    def _():
        cn = cn_ref[0].astype(BF16)
        s_n = (jnp.dot(cn, ql, preferred_element_type=F32)
               + jnp.dot(kn_ref[0].astype(BF16), qp, preferred_element_type=F32))
        t_key = lax.broadcasted_iota(jnp.int32, s_n.shape, 0)
        t_query = lax.broadcasted_iota(jnp.int32, s_n.shape, 1) // (n_q // steps)
        update(jnp.where((t_key <= t_query) & (t_key < steps), s_n, -jnp.inf), cn)
        o_ref[0] = (acc_sc[...] / per_query_rows(l_sc[...])).astype(o_ref.dtype)


def mla_sample_attention(q_lat, q_pe, ckv_new, kpe_new, pool_ckv, pool_kpe_t, layer, page_table, steps):
    db, _, rows = q_lat.shape
    n_pages = page_table.shape[1]
    page = pool_ckv.shape[2]
    pps = min(32, n_pages)
    assert n_pages % pps == 0 and rows == LANES
    new_pad = 16
    pad_new = lambda x: jnp.pad(x, ((0, 0), (0, new_pad - steps), (0, 0)))

    def page_spec(shape, slot):
        return pl.BlockSpec((1, 1) + shape, lambda b, c, pt: (layer, pt[b, c * pps + slot], 0, 0))

    per_seq = lambda n, w: pl.BlockSpec((1, n, w), lambda b, c, pt: (b, 0, 0))
    grid_spec = pltpu.PrefetchScalarGridSpec(
        num_scalar_prefetch=1,
        grid=(db, n_pages // pps),
        in_specs=([per_seq(KV_LORA, rows), per_seq(QK_ROPE, rows), per_seq(new_pad, KV_LORA), per_seq(new_pad, QK_ROPE)]
                  + [page_spec((page, KV_LORA), k) for k in range(pps)]
                  + [page_spec((QK_ROPE, page), k) for k in range(pps)]),
        out_specs=per_seq(rows, KV_LORA),
        scratch_shapes=[pltpu.VMEM((1, rows), F32), pltpu.VMEM((1, rows), F32),
                        pltpu.VMEM((rows, KV_LORA), F32), pltpu.VMEM((pps * page, KV_LORA), BF16),
                        pltpu.VMEM((QK_ROPE, pps * page), BF16)],
    )
    return pl.pallas_call(
        functools.partial(_mla_sample_kernel, pages_per_step=pps, page=page, steps=steps),
        grid_spec=grid_spec,
        out_shape=jax.ShapeDtypeStruct((db, rows, KV_LORA), BF16),
        compiler_params=_params(2),
        name="mla_sample_attention",
    )(page_table, q_lat, q_pe, pad_new(ckv_new), pad_new(kpe_new), *([pool_ckv] * pps), *([pool_kpe_t] * pps))


def _rope_angles(pos, dim):
    inv = ROPE_THETA ** (-jnp.arange(0, dim, 2, dtype=F32) / dim)
    ang = pos.astype(F32)[:, None] * inv[None, :]
    return jnp.cos(ang), jnp.sin(ang)


def _head_rope_tables(pos):
    c, s = _rope_angles(pos, HEAD_DIM)
    cos = jnp.stack([jnp.concatenate([c, c], 1), jnp.ones((pos.shape[0], LANES), F32)])
    sin = jnp.stack([jnp.concatenate([-s, s], 1), jnp.zeros((pos.shape[0], LANES), F32)])
    return cos, sin


def _mla_rope_tables(pos, scale=1.0):
    c, s = _rope_angles(pos, QK_ROPE)
    z = jnp.zeros_like(c)
    cos = jnp.stack([jnp.concatenate([c, z, c, z], 1), jnp.ones((pos.shape[0], LANES), F32)])
    sin = jnp.stack([jnp.concatenate([-s, z, s, z], 1), jnp.zeros((pos.shape[0], LANES), F32)])
    return cos * scale, sin * scale


def _spread_rope_cols(w):
    half = QK_ROPE // 2
    z = jnp.zeros(w.shape[:-1] + (half,), w.dtype)
    return jnp.concatenate([w[..., :half], z, w[..., half:], z], axis=-1)


def _gather_rope_cols(x):
    half = QK_ROPE // 2
    return jnp.concatenate([x[..., :half], x[..., 2 * half:3 * half]], axis=-1)


EVEN_HEAD_TYPES = np.array([0] * (HQ_A + HKV_A) + [1] * HKV_A + [0] * (HQ_B + HKV_B) + [1] * HKV_B, np.int32)
MLA_Q_HEAD_TYPES = np.array([1, 0] * H_C, np.int32)
COL_QA, COL_KA, COL_VA = 0, HQ_A, HQ_A + HKV_A
COL_QB, COL_KB, COL_VB = HQ_A + 2 * HKV_A, HQ_A + 2 * HKV_A + HQ_B, HQ_A + 2 * HKV_A + HQ_B + HKV_B


def _cols(x, first_head, n_heads):
    return x[:, first_head * HEAD_DIM:(first_head + n_heads) * HEAD_DIM]


def kernel(x_prompt, x_sample, cache_a_k, cache_a_v, cache_b_k, cache_b_v, cache_mla_ckv, cache_mla_kpe,
           page_table, norm_mix_pre, norm_mix_post, norm_ffn_pre, norm_ffn_post, w_in_e, sink_b, w_out_e,
           w_dq, q_norm, w_uq, w_dkv, kv_norm, w_ukv, w_o_mla, w_up, w_down):
    batch, seq, d_model = x_prompt.shape
    db, steps, _ = x_sample.shape
    depth = norm_mix_pre.shape[0]
    past_len = page_table.shape[1] * cache_mla_ckv.shape[2]
    keep_a = min(max(w for (w, _) in A_PATTERNS), seq)
    keep_b = min(WIN_B, seq)

    pos_p = jnp.arange(seq)
    pos_s = jnp.tile(past_len + jnp.arange(steps), db)
    rope_p, rope_s = _head_rope_tables(pos_p), _head_rope_tables(pos_s)
    mrope_p, mrope_s = _mla_rope_tables(pos_p), _mla_rope_tables(pos_s)
    mq_p, mq_s = _mla_rope_tables(pos_p, MLA_SCALE * LOG2E), _mla_rope_tables(pos_s, MLA_SCALE * LOG2E)
    even_types = jnp.asarray(EVEN_HEAD_TYPES)
    mla_q_types = jnp.asarray(MLA_Q_HEAD_TYPES)
    w_kv_b = w_ukv.astype(BF16)
    pool_kpe_t = jnp.swapaxes(cache_mla_kpe, 2, 3)

    hp = x_prompt.reshape(batch * seq, d_model)
    hs = x_sample.reshape(db * steps, d_model)
    xp = norm_cast(hp, norm_mix_pre[0])
    xs = norm_cast(hs, norm_mix_pre[0])
    outs = {name: [] for name in ("p_ak", "p_av", "p_bk", "p_bv", "p_ckv", "p_kpe",
                                  "s_ak", "s_av", "s_bk", "s_bv", "s_ckv", "s_kpe")}

    for layer in range(depth):
        li = layer // 2
        if layer % 2 == 0:
            proj, w_in_l = matmul_rope(xs, w_in_e, li, even_types, *rope_s, F32, emit_bf16=True)
            per_tok = lambda x, n: x.reshape(db, steps, n, HEAD_DIM)
            ka, va = per_tok(_cols(proj, COL_KA, HKV_A), HKV_A), per_tok(_cols(proj, COL_VA, HKV_A), HKV_A)
            kb, vb = per_tok(_cols(proj, COL_KB, HKV_B), HKV_B), per_tok(_cols(proj, COL_VB, HKV_B), HKV_B)
            oa = sample_window_attention(per_tok(_cols(proj, COL_QA, HQ_A), HQ_A), ka, va, cache_a_k, cache_a_v,
                                         li, HKV_A, HQ_A // HKV_A, A_PATTERNS, None)
            ob = sample_window_attention(per_tok(_cols(proj, COL_QB, HQ_B), HQ_B), kb, vb, cache_b_k, cache_b_v,
                                         li, HKV_B, HQ_B // HKV_B, ((WIN_B, 1),), sink_b[li])
            mix_s, w_out_l = matmul([oa, ob], w_out_e, li, F32, emit_bf16=True)
            outs["s_ak"].append(ka); outs["s_av"].append(va); outs["s_bk"].append(kb); outs["s_bv"].append(vb)
            proj = matmul_rope(xp, w_in_l, 0, even_types, *rope_p, F32)
            oa = band_attention(proj, seq, COL_QA, COL_KA, COL_VA, HKV_A, HQ_A // HKV_A, A_PATTERNS, None, HKV_A)
            ob = band_attention(proj, seq, COL_QB, COL_KB, COL_VB, HKV_B, HQ_B // HKV_B, ((WIN_B, 1),),
                                sink_b[li], 1)
            mix_p = matmul([oa, ob], w_out_l, 0, F32)
            per_seq = lambda x, n: x.reshape(batch, seq, n, HEAD_DIM)
            outs["p_ak"].append(per_seq(_cols(proj, COL_KA, HKV_A), HKV_A)[:, seq - keep_a:])
            outs["p_av"].append(per_seq(_cols(proj, COL_VA, HKV_A), HKV_A)[:, seq - keep_a:])
            outs["p_bk"].append(per_seq(_cols(proj, COL_KB, HKV_B), HKV_B)[:, seq - keep_b:])
            outs["p_bv"].append(per_seq(_cols(proj, COL_VB, HKV_B), HKV_B)[:, seq - keep_b:])
        else:
            w_cat = jnp.concatenate([w_dq[li], w_dkv[li][:, :KV_LORA], _spread_rope_cols(w_dkv[li][:, KV_LORA:])],
                                    axis=1).astype(BF16)
            w_uq_l = w_uq[li].reshape(Q_LORA, H_C, QK_NOPE + QK_ROPE)
            w_q = jnp.concatenate([w_uq_l[..., :QK_NOPE], _spread_rope_cols(w_uq_l[..., QK_NOPE:])], axis=-1)
            w_q = w_q.reshape(1, Q_LORA, H_C * 2 * LANES).astype(BF16)
            cq, ckv, kpe = mla_down(xs, w_cat, q_norm[li], kv_norm[li], mrope_s[0][0], mrope_s[1][0])
            q_full = matmul_rope(cq, w_q, 0, mla_q_types, *mq_s, BF16, tn=1024)
            q_lat = head_matmul(q_full, w_kv_b, li, LANES, 2, 0, KV_LORA, True)
            q_pe = _gather_rope_cols(q_full.reshape(db * steps, H_C, 2, LANES)[:, :, 1, :])
            ckv_s = ckv.reshape(db, steps, KV_LORA)
            kpe_s = _gather_rope_cols(kpe).reshape(db, steps, QK_ROPE)
            queries_as_cols = lambda x, w: x.reshape(db, steps * H_C, w).transpose(0, 2, 1)
            o_lat = mla_sample_attention(queries_as_cols(q_lat, KV_LORA), queries_as_cols(q_pe, QK_ROPE), ckv_s, kpe_s,
                                         cache_mla_ckv, pool_kpe_t, li, page_table, steps)
            o = head_matmul(o_lat.reshape(db * steps, H_C * KV_LORA), w_kv_b, li, KV_LORA, 1, 1, V_DIM, False)
            mix_s, w_o_l = matmul([o], w_o_mla, li, F32, emit_bf16=True)
            outs["s_ckv"].append(ckv_s); outs["s_kpe"].append(kpe_s)
            cq, ckv, kpe = mla_down(xp, w_cat, q_norm[li], kv_norm[li], mrope_p[0][0], mrope_p[1][0])
            q_full = matmul_rope(cq, w_q, 0, mla_q_types, *mq_p, BF16, tn=1024)
            kv = matmul([ckv], w_kv_b, li, BF16, tn=1024)
            o = mla_prompt_attention(q_full, kv, kpe.astype(BF16), seq)
            mix_p = matmul([o], w_o_l, 0, F32)
            outs["p_ckv"].append(ckv.reshape(batch, seq, KV_LORA))
            outs["p_kpe"].append(_gather_rope_cols(kpe).reshape(batch, seq, QK_ROPE))

        hp, xp = resid_norm(hp, mix_p, norm_mix_post[layer], norm_ffn_pre[layer])
        hs, xs = resid_norm(hs, mix_s, norm_mix_post[layer], norm_ffn_pre[layer])
        g_next = norm_mix_pre[layer + 1] if layer + 1 < depth else None
        mlp_s, w_u, w_d = mlp_cast(xs, w_up, w_down, layer)
        hs, xs = resid_norm(hs, mlp_s, norm_ffn_post[layer], g_next)
        hp, xp = resid_norm(hp, mlp(xp, w_u, w_d), norm_ffn_post[layer], g_next)

    stack = lambda name: jnp.stack(outs[name])
    return (hp.reshape(batch, seq, d_model), hs.reshape(db, steps, d_model),
            stack("p_ak"), stack("p_av"), stack("p_bk"), stack("p_bv"), stack("p_ckv"), stack("p_kpe"),
            stack("s_ak"), stack("s_av"), stack("s_bk"), stack("s_bv"), stack("s_ckv"), stack("s_kpe"))
```
